```python
import jax, jax.numpy as jnp
from jax import lax
import numpy as np

D_MODEL = 2048
BATCH = 2
SEQ = 16384
DEPTH = 1

MEM_LEN = 256
HEAD_DIM = 128
MOBA_HEADS = 8
MOBA_W = MOBA_HEADS * HEAD_DIM
MOBA_BLOCK = 256
MOBA_TOPK = 3
MOBA_Q_CHUNK = 64
GMLP_GROUPS = 4
GMLP_W = GMLP_GROUPS * HEAD_DIM
GMLP_CHUNK = 128
MEM_HEADS = 4
MEM_W = MEM_HEADS * HEAD_DIM
D_MIX = MOBA_W + GMLP_W + MEM_W
IN_SPLITS = (MOBA_W, MOBA_W, MOBA_W, MOBA_W, GMLP_W, GMLP_W, GMLP_W, MEM_W, MEM_W)
D_IN = sum(IN_SPLITS)
IN_SPLIT_POINTS = tuple(int(c) for c in np.cumsum(IN_SPLITS)[:-1])
ROPE_THETA = 10000.0
LN_EPS = 1e-5
DEEPNORM_ALPHA = (2.0 * DEPTH) ** 0.25
DEEPNORM_BETA = (8.0 * DEPTH) ** -0.25
NEG_INF = -1e30

kernel_name = "hymba_moba_gmlp_memxattn_deepnorm"


def layer_norm(x, g, b):
    xf = x.astype(jnp.float32)
    mu = xf.mean(-1, keepdims=True)
    var = jnp.square(xf - mu).mean(-1, keepdims=True)
    return ((xf - mu) * lax.rsqrt(var + LN_EPS) * g.astype(jnp.float32) + b.astype(jnp.float32)).astype(x.dtype)


def rope(x, positions):
    half = HEAD_DIM // 2
    inv_freq = ROPE_THETA ** (-jnp.arange(half, dtype=jnp.float32) / half)
    ang = positions.astype(jnp.float32)[..., None] * inv_freq
    cos = jnp.cos(ang)[:, :, None, :]
    sin = jnp.sin(ang)[:, :, None, :]
    xf = x.astype(jnp.float32)
    x1, x2 = xf[..., :half], xf[..., half:]
    return jnp.concatenate([x1 * cos - x2 * sin, x2 * cos + x1 * sin], axis=-1).astype(x.dtype)


def moba_attention(q, k, v):
    b, s, h, d = q.shape
    s_pad = -(-s // MOBA_BLOCK) * MOBA_BLOCK
    pad = ((0, 0), (0, s_pad - s), (0, 0), (0, 0))
    q, k, v = (jnp.pad(t, pad).transpose(0, 2, 1, 3) for t in (q, k, v))
    nb = s_pad // MOBA_BLOCK
    n_sel = min(MOBA_TOPK, nb)
    kb = k.reshape(b, h, nb, MOBA_BLOCK, d)
    vb = v.reshape(b, h, nb, MOBA_BLOCK, d)
    k_mean = kb.astype(jnp.float32).mean(axis=3)
    q_block = jnp.arange(s_pad) // MOBA_BLOCK
    gate = jnp.einsum('bhsd,bhnd->bhsn', q.astype(jnp.float32), k_mean)
    past = jnp.arange(nb)[None, :] < q_block[:, None]
    gate = jnp.where(past, gate, NEG_INF)
    _, sel_idx = lax.top_k(gate, n_sel)
    sel_valid = sel_idx < q_block[:, None]
    scale = d ** -0.5
    bi = jnp.arange(b)[:, None, None, None]
    hi = jnp.arange(h)[None, :, None, None]
    key_off = jnp.arange(MOBA_BLOCK)
    q_off = jnp.arange(MOBA_Q_CHUNK)
    n_keys_sel = n_sel * MOBA_BLOCK

    def chunk(c):
        start = c * MOBA_Q_CHUNK
        q_c = lax.dynamic_slice_in_dim(q, start, MOBA_Q_CHUNK, axis=2)
        idx_c = lax.dynamic_slice_in_dim(sel_idx, start, MOBA_Q_CHUNK, axis=2)
        val_c = lax.dynamic_slice_in_dim(sel_valid, start, MOBA_Q_CHUNK, axis=2)
        blk = start // MOBA_BLOCK
        k_own = lax.dynamic_index_in_dim(kb, blk, axis=2, keepdims=False)
        v_own = lax.dynamic_index_in_dim(vb, blk, axis=2, keepdims=False)
        k_sel = kb[bi, hi, idx_c]
        v_sel = vb[bi, hi, idx_c]
        l_sel = jnp.einsum('bhqd,bhqnkd->bhqnk', q_c, k_sel).astype(jnp.float32) * scale
        l_sel = jnp.where(val_c[..., None], l_sel, NEG_INF).reshape(b, h, MOBA_Q_CHUNK, n_keys_sel)
        l_own = jnp.einsum('bhqd,bhkd->bhqk', q_c, k_own).astype(jnp.float32) * scale
        causal = (blk * MOBA_BLOCK + key_off)[None, :] <= (start + q_off)[:, None]
        l_own = jnp.where(causal, l_own, NEG_INF)
        p = jax.nn.softmax(jnp.concatenate([l_sel, l_own], axis=-1), axis=-1).astype(v.dtype)
        p_sel = p[..., :n_keys_sel].reshape(b, h, MOBA_Q_CHUNK, n_sel, MOBA_BLOCK)
        p_own = p[..., n_keys_sel:]
        return (jnp.einsum('bhqnk,bhqnkd->bhqd', p_sel, v_sel)
                + jnp.einsum('bhqk,bhkd->bhqd', p_own, v_own))

    out = lax.map(chunk, jnp.arange(s_pad // MOBA_Q_CHUNK))
    out = out.transpose(1, 0, 3, 2, 4).reshape(b, s_pad, h, d)
    return out[:, :s]


def gmlp_spatial_gate(u, v, ln_g, ln_b, w_s, b_s):
    u = jax.nn.gelu(u)
    v = layer_norm(jax.nn.gelu(v), ln_g, ln_b)
    b, s, _ = v.shape
    v = v.reshape(b, s // GMLP_CHUNK, GMLP_CHUNK, GMLP_GROUPS, HEAD_DIM)
    tril = jnp.tril(jnp.ones((GMLP_CHUNK, GMLP_CHUNK), dtype=bool))
    w = jnp.where(tril, w_s, 0)
    mixed = jnp.einsum('gts,bnsgc->bntgc', w, v) + b_s.T[None, None, :, :, None]
    return u * mixed.reshape(b, s, GMLP_W)


def memory_attention(q, mem_k, mem_v):
    logits = jnp.einsum('bshd,bmhd->bhsm', q, mem_k).astype(jnp.float32) * (HEAD_DIM ** -0.5)
    p = jax.nn.softmax(logits, axis=-1).astype(mem_v.dtype)
    return jnp.einsum('bhsm,bmhd->bshd', p, mem_v)


def setup_inputs(seed: int = 0) -> dict:
    key = jax.random.key(seed)
    ks = jax.random.split(key, 12)
    f32 = jnp.float32
    x = jax.random.normal(ks[0], (BATCH, SEQ, D_MODEL), f32)
    mem = jax.random.normal(ks[1], (BATCH, MEM_LEN, D_MODEL), f32)
    positions = jnp.broadcast_to(jnp.arange(SEQ, dtype=jnp.int32), (BATCH, SEQ))
    w_in = jax.random.normal(ks[2], (DEPTH, D_MODEL, D_IN), f32) * D_MODEL ** -0.5
    w_mem_kv = jax.random.normal(ks[3], (DEPTH, D_MODEL, 2 * MEM_W), f32) * D_MODEL ** -0.5
    gmlp_ln_g = 1.0 + 0.02 * jax.random.normal(ks[4], (DEPTH, GMLP_W), f32)
    gmlp_ln_b = 0.02 * jax.random.normal(ks[5], (DEPTH, GMLP_W), f32)
    gmlp_w_s = jax.random.normal(ks[6], (DEPTH, GMLP_GROUPS, GMLP_CHUNK, GMLP_CHUNK), f32) * GMLP_CHUNK ** -0.5
    gmlp_b_s = 1.0 + 0.1 * jax.random.normal(ks[7], (DEPTH, GMLP_GROUPS, GMLP_CHUNK), f32)
    w_out = jax.random.normal(ks[8], (DEPTH, D_MIX, D_MODEL), f32) * (D_MIX ** -0.5) * DEEPNORM_BETA
    ln_g = 1.0 + 0.02 * jax.random.normal(ks[9], (DEPTH, D_MODEL), f32)
    ln_b = 0.02 * jax.random.normal(ks[10], (DEPTH, D_MODEL), f32)
    return {"x": x, "mem": mem, "positions": positions, "w_in": w_in, "w_mem_kv": w_mem_kv,
            "gmlp_ln_g": gmlp_ln_g, "gmlp_ln_b": gmlp_ln_b, "gmlp_w_s": gmlp_w_s,
            "gmlp_b_s": gmlp_b_s, "w_out": w_out, "ln_g": ln_g, "ln_b": ln_b}


def reference(x, mem, positions, w_in, w_mem_kv, gmlp_ln_g, gmlp_ln_b, gmlp_w_s, gmlp_b_s,
              w_out, ln_g, ln_b):
    b, s, _ = x.shape
    for l in range(DEPTH):
        proj = jnp.einsum('bsd,de->bse', x, w_in[l])
        q_mo, k_mo, v_mo, g_mo, u_gm, v_gm, g_gm, q_me, g_me = jnp.split(proj, IN_SPLIT_POINTS, axis=-1)
        q_mo = rope(q_mo.reshape(b, s, MOBA_HEADS, HEAD_DIM), positions)
        k_mo = rope(k_mo.reshape(b, s, MOBA_HEADS, HEAD_DIM), positions)
        v_mo = v_mo.reshape(b, s, MOBA_HEADS, HEAD_DIM)
        y_mo = moba_attention(q_mo, k_mo, v_mo).reshape(b, s, MOBA_W) * jax.nn.silu(g_mo)
        y_gm = gmlp_spatial_gate(u_gm, v_gm, gmlp_ln_g[l], gmlp_ln_b[l], gmlp_w_s[l], gmlp_b_s[l]) * jax.nn.silu(g_gm)
        mem_k, mem_v = jnp.split(jnp.einsum('bmd,de->bme', mem, w_mem_kv[l]), 2, axis=-1)
        y_me = memory_attention(q_me.reshape(b, s, MEM_HEADS, HEAD_DIM),
                                mem_k.reshape(b, MEM_LEN, MEM_HEADS, HEAD_DIM),
                                mem_v.reshape(b, MEM_LEN, MEM_HEADS, HEAD_DIM)).reshape(b, s, MEM_W) * jax.nn.silu(g_me)
        y = jnp.concatenate([y_mo, y_gm, y_me], axis=-1)
        sub = jnp.einsum('bse,ed->bsd', y, w_out[l])
        x = layer_norm(DEEPNORM_ALPHA * x + sub, ln_g[l], ln_b[l])
    return x
```

```python
import functools

import jax
import jax.numpy as jnp
import numpy as np
from jax import lax
from jax.experimental import pallas as pl
from jax.experimental.pallas import tpu as pltpu

D_MODEL = 2048
HEAD_DIM = 128
MOBA_HEADS = 8
MOBA_W = MOBA_HEADS * HEAD_DIM
MOBA_BLOCK = 256
MOBA_TOPK = 3
GMLP_GROUPS = 4
GMLP_W = GMLP_GROUPS * HEAD_DIM
GMLP_CHUNK = 128
MEM_HEADS = 4
MEM_W = MEM_HEADS * HEAD_DIM
MEM_LEN = 256
D_MIX = MOBA_W + GMLP_W + MEM_W
D_IN = 4 * MOBA_W + 3 * GMLP_W + 2 * MEM_W
ROPE_THETA = 10000.0
LN_EPS = 1e-5
NEG_INF = -1e30
ATTN_SCALE = HEAD_DIM ** -0.5

LANES = 128
PROJ_BLOCK = 512
N_PROJ_BLOCKS = D_IN // PROJ_BLOCK
PB_Q, PB_K, PB_V, PB_G = 0, 2, 4, 6
PB_U_GM, PB_V_GM, PB_G_GM, PB_Q_ME, PB_G_ME = 8, 9, 10, 11, 12

INPROJ_ROWS = 1024
TAIL_ROWS = 256
VMEM_LIMIT = 48 * 1024 * 1024

F32 = jnp.float32
BF16 = jnp.bfloat16

_NT = (((1,), (1,)), ((), ()))


def _dot(a, b):
    return jnp.dot(a, b, preferred_element_type=F32)


def _dot_nt(a, b):
    return lax.dot_general(a, b, _NT, preferred_element_type=F32)


def _lane_tile(a, reps):
    return a if reps == 1 else jnp.concatenate([a] * reps, axis=1)


def _silu(y):
    return y * (1.0 / (1.0 + jnp.exp(-y)))


def _gelu_tanh(y):
    c = np.float32(np.sqrt(2.0 / np.pi))
    return 0.5 * y * (1.0 + jnp.tanh(c * (y + 0.044715 * (y * y * y))))


def _memkv_kernel(mem_ref, wk_ref, wvt_ref, k_ref, vt_ref):
    m = mem_ref[0].astype(BF16)
    k_ref[0] = _dot(m, wk_ref[...]).astype(BF16)
    vt_ref[0] = _dot_nt(wvt_ref[...], m).astype(BF16)


def _memkv(mem, wk, wvt):
    b = mem.shape[0]
    return pl.pallas_call(
        _memkv_kernel,
        grid=(b,),
        in_specs=[
            pl.BlockSpec((1, MEM_LEN, D_MODEL), lambda i: (i, 0, 0)),
            pl.BlockSpec((D_MODEL, MEM_W), lambda i: (0, 0)),
            pl.BlockSpec((MEM_W, D_MODEL), lambda i: (0, 0)),
        ],
        out_specs=[
            pl.BlockSpec((1, MEM_LEN, MEM_W), lambda i: (i, 0, 0)),
            pl.BlockSpec((1, MEM_W, MEM_LEN), lambda i: (i, 0, 0)),
        ],
        out_shape=[
            jax.ShapeDtypeStruct((b, MEM_LEN, MEM_W), BF16),
            jax.ShapeDtypeStruct((b, MEM_W, MEM_LEN), BF16),
        ],
        compiler_params=pltpu.CompilerParams(
            dimension_semantics=("arbitrary",), vmem_limit_bytes=VMEM_LIMIT),
        name="memkv",
    )(mem, wk, wvt)


def _inproj_kernel(pos_ref, x_ref, w_ref, inv_ref, lng_ref, lnb_ref, o_ref,
                   xb_scr, cos_scr, sin_scr):
    j = pl.program_id(1)
    tm = x_ref.shape[0]
    reps = tm // LANES

    @pl.when(j == 0)
    def _():
        xb_scr[...] = x_ref[...].astype(BF16)
        pos = pos_ref[0].astype(F32)
        ang = _lane_tile(inv_ref[...], reps) * pos
        cos_scr[...] = jnp.cos(ang)
        sin_scr[...] = jnp.sin(ang)

    y = _dot_nt(w_ref[...], xb_scr[...])

    def rope_store(scale):
        half = HEAD_DIM // 2
        c = cos_scr[...]
        s = sin_scr[...]
        for h in range(PROJ_BLOCK // HEAD_DIM):
            x1 = y[h * HEAD_DIM:h * HEAD_DIM + half]
            x2 = y[h * HEAD_DIM + half:(h + 1) * HEAD_DIM]
            o1 = x1 * c - x2 * s
            o2 = x2 * c + x1 * s
            if scale != 1.0:
                o1 = o1 * scale
                o2 = o2 * scale
            o_ref[0, h * HEAD_DIM:h * HEAD_DIM + half, :] = o1.astype(BF16)
            o_ref[0, h * HEAD_DIM + half:(h + 1) * HEAD_DIM, :] = o2.astype(BF16)

    @pl.when(j < PB_K)
    def _():
        rope_store(ATTN_SCALE)

    @pl.when(jnp.logical_and(j >= PB_K, j < PB_V))
    def _():
        rope_store(1.0)

    @pl.when(jnp.logical_and(j >= PB_V, j < PB_G))
    def _():
        o_ref[0] = y.astype(BF16)

    is_gate = jnp.logical_or(jnp.logical_and(j >= PB_G, j < PB_U_GM),
                             jnp.logical_or(j == PB_G_GM, j == PB_G_ME))

    @pl.when(is_gate)
    def _():
        o_ref[0] = _silu(y).astype(BF16)

    @pl.when(j == PB_U_GM)
    def _():
        o_ref[0] = _gelu_tanh(y).astype(BF16)

    @pl.when(j == PB_V_GM)
    def _():
        v = _gelu_tanh(y)
        mu = jnp.mean(v, axis=0, keepdims=True)
        d = v - mu
        var = jnp.mean(d * d, axis=0, keepdims=True)
        n = d * lax.rsqrt(var + LN_EPS)
        o_ref[0] = (n * _lane_tile(lng_ref[...], reps) + _lane_tile(lnb_ref[...], reps)).astype(BF16)

    @pl.when(j == PB_Q_ME)
    def _():
        o_ref[0] = (y * ATTN_SCALE).astype(BF16)


def _inproj(pos3, x2, w_t, inv_freq, lng, lnb, batch, seq):
    tm = INPROJ_ROWS
    nt = seq // tm
    return pl.pallas_call(
        _inproj_kernel,
        grid=(batch * nt, N_PROJ_BLOCKS),
        in_specs=[
            pl.BlockSpec((1, 1, tm), lambda i, j: (i, 0, 0)),
            pl.BlockSpec((tm, D_MODEL), lambda i, j: (i, 0)),
            pl.BlockSpec((PROJ_BLOCK, D_MODEL), lambda i, j: (j, 0)),
            pl.BlockSpec((HEAD_DIM // 2, LANES), lambda i, j: (0, 0)),
            pl.BlockSpec((GMLP_W, LANES), lambda i, j: (0, 0)),
            pl.BlockSpec((GMLP_W, LANES), lambda i, j: (0, 0)),
        ],
        out_specs=pl.BlockSpec((1, PROJ_BLOCK, tm), lambda i, j: (i // nt, j, i % nt)),
        out_shape=jax.ShapeDtypeStruct((batch, D_IN, seq), BF16),
        scratch_shapes=[
            pltpu.VMEM((tm, D_MODEL), BF16),
            pltpu.VMEM((HEAD_DIM // 2, tm), F32),
            pltpu.VMEM((HEAD_DIM // 2, tm), F32),
        ],
        compiler_params=pltpu.CompilerParams(
            dimension_semantics=("arbitrary", "arbitrary"), vmem_limit_bytes=VMEM_LIMIT),
        name="inproj",
    )(pos3, x2, w_t, inv_freq, lng, lnb)


def _moba_kernel(q_ref, kt_ref, vt_ref, g_ref, o_ref, k_scr, km_scr, qa_scr):
    i = pl.program_id(2)
    nb = kt_ref.shape[2] // MOBA_BLOCK
    blk = MOBA_BLOCK

    @pl.when(i == 0)
    def _():
        col = lax.broadcasted_iota(jnp.int32, (blk, LANES), 1)

        def body(c, carry):
            start = pl.multiple_of(c * blk, blk)
            k = kt_ref[0, :, pl.ds(start, blk)].astype(F32).T
            k_scr[pl.ds(start, blk), 0:HEAD_DIM] = k.astype(BF16)
            k_scr[pl.ds(start, blk), HEAD_DIM:2 * HEAD_DIM] = jnp.where(col == c, 1.0, 0.0).astype(BF16)
            km_scr[pl.ds(c, 1), :] = jnp.sum(k, axis=0, keepdims=True) * (1.0 / blk)
            return carry

        lax.fori_loop(0, nb, body, 0)

    qt = q_ref[0]
    km = km_scr[...]
    km_hi = km.astype(BF16)
    km_lo = (km - km_hi.astype(F32)).astype(BF16)
    gate = _dot(km_hi, qt) + _dot(km_lo, qt)
    bid = lax.broadcasted_iota(jnp.int32, gate.shape, 0)
    past = bid < i
    g = jnp.where(past, gate, NEG_INF)
    sel = jnp.zeros(gate.shape, F32)
    for _ in range(MOBA_TOPK):
        mx = jnp.max(g, axis=0, keepdims=True)
        first = jnp.min(jnp.where(g == mx, bid, nb), axis=0, keepdims=True)
        hit = bid == first
        sel = jnp.where(jnp.logical_and(hit, past), 1.0, sel)
        g = jnp.where(hit, -jnp.inf, g)
    bias = jnp.where(jnp.logical_or(sel > 0.0, bid == i), 0.0, NEG_INF)
    qa_scr[0:HEAD_DIM, :] = qt
    qa_scr[HEAD_DIM:HEAD_DIM + nb, :] = bias.astype(BF16)
    if HEAD_DIM + nb < 2 * HEAD_DIM:
        qa_scr[HEAD_DIM + nb:2 * HEAD_DIM, :] = jnp.zeros((HEAD_DIM - nb, blk), BF16)
    qa = qa_scr[...]

    own = pl.multiple_of(i * blk, blk)
    s = _dot(k_scr[pl.ds(own, blk), :], qa)
    kk = lax.broadcasted_iota(jnp.int32, s.shape, 0)
    qq = lax.broadcasted_iota(jnp.int32, s.shape, 1)
    s = jnp.where(kk <= qq, s, NEG_INF)
    m0 = jnp.max(s, axis=0, keepdims=True)
    p = jnp.exp(s - m0)
    l0 = jnp.sum(p, axis=0, keepdims=True)
    acc0 = _dot(vt_ref[0, :, pl.ds(own, blk)], p.astype(BF16))

    def body(j, carry):
        m, l, acc = carry
        start = pl.multiple_of(j * blk, blk)
        s = _dot(k_scr[pl.ds(start, blk), :], qa)
        m_new = jnp.maximum(m, jnp.max(s, axis=0, keepdims=True))
        alpha = jnp.exp(m - m_new)
        p = jnp.exp(s - m_new)
        l = alpha * l + jnp.sum(p, axis=0, keepdims=True)
        acc = alpha * acc + _dot(vt_ref[0, :, pl.ds(start, blk)], p.astype(BF16))
        return m_new, l, acc

    _, l, acc = lax.fori_loop(0, i, body, (m0, l0, acc0))
    o_ref[0] = (acc / l * g_ref[0].astype(F32)).astype(BF16)


def _moba(proj_t, batch, seq):
    nq = seq // MOBA_BLOCK
    nb = nq
    assert nb <= HEAD_DIM, "block mask rows must fit beside the head dim in one 256-deep contraction"
    hb = PROJ_BLOCK // HEAD_DIM
    return pl.pallas_call(
        _moba_kernel,
        grid=(batch, MOBA_HEADS, nq),
        in_specs=[
            pl.BlockSpec((1, HEAD_DIM, MOBA_BLOCK), lambda b, h, i: (b, PB_Q * hb + h, i)),
            pl.BlockSpec((1, HEAD_DIM, seq), lambda b, h, i: (b, PB_K * hb + h, 0)),
            pl.BlockSpec((1, HEAD_DIM, seq), lambda b, h, i: (b, PB_V * hb + h, 0)),
            pl.BlockSpec((1, HEAD_DIM, MOBA_BLOCK), lambda b, h, i: (b, PB_G * hb + h, i)),
        ],
        out_specs=pl.BlockSpec((1, HEAD_DIM, MOBA_BLOCK), lambda b, h, i: (b, h, i)),
        out_shape=jax.ShapeDtypeStruct((batch, MOBA_W, seq), BF16),
        scratch_shapes=[
            pltpu.VMEM((seq, 2 * HEAD_DIM), BF16),
            pltpu.VMEM((nb, HEAD_DIM), F32),
            pltpu.VMEM((2 * HEAD_DIM, MOBA_BLOCK), BF16),
        ],
        compiler_params=pltpu.CompilerParams(
            dimension_semantics=("arbitrary", "arbitrary", "arbitrary"),
            vmem_limit_bytes=VMEM_LIMIT),
        name="moba",
    )(proj_t, proj_t, proj_t, proj_t)


def _tail_kernel(u_ref, v_ref, gg_ref, qm_ref, gm_ref, ymo_ref, x_ref, mk_ref, mvt_ref,
                 wst_ref, bs_ref, wot_ref, lng_ref, lnb_ref, o_ref, y_scr, *, alpha):
    tm = x_ref.shape[0]
    nchunk = tm // GMLP_CHUNK
    hd = HEAD_DIM

    y_scr[0:MOBA_W, :] = ymo_ref[0]

    rr = lax.broadcasted_iota(jnp.int32, (GMLP_CHUNK, GMLP_CHUNK), 0)
    cc = lax.broadcasted_iota(jnp.int32, (GMLP_CHUNK, GMLP_CHUNK), 1)
    for g in range(GMLP_GROUPS):
        rows = slice(g * hd, (g + 1) * hd)
        w_t = jnp.where(rr <= cc, wst_ref[g], 0.0).astype(BF16)
        vg = v_ref[0, rows, :]
        stacked = jnp.concatenate(
            [vg[:, n * GMLP_CHUNK:(n + 1) * GMLP_CHUNK] for n in range(nchunk)], axis=0)
        mixed = _dot(stacked, w_t)
        mixed_t = jnp.concatenate(
            [mixed[n * hd:(n + 1) * hd, :] for n in range(nchunk)], axis=1)
        mixed_t = mixed_t + _lane_tile(bs_ref[g:g + 1, :], nchunk)
        yg = u_ref[0, rows, :].astype(F32) * mixed_t * gg_ref[0, rows, :].astype(F32)
        y_scr[MOBA_W + g * hd:MOBA_W + (g + 1) * hd, :] = yg.astype(BF16)

    for h in range(MEM_HEADS):
        rows = slice(h * hd, (h + 1) * hd)
        s = _dot(mk_ref[0, :, rows], qm_ref[0, rows, :])
        m = jnp.max(s, axis=0, keepdims=True)
        p = jnp.exp(s - m)
        l = jnp.sum(p, axis=0, keepdims=True)
        o = _dot(mvt_ref[0, rows, :], p.astype(BF16))
        ym = o / l * gm_ref[0, rows, :].astype(F32)
        y_scr[MOBA_W + GMLP_W + h * hd:MOBA_W + GMLP_W + (h + 1) * hd, :] = ym.astype(BF16)

    sub = _dot(wot_ref[...], y_scr[...]).T
    z = alpha * x_ref[...] + sub
    mu = jnp.mean(z, axis=-1, keepdims=True)
    d = z - mu
    var = jnp.mean(d * d, axis=-1, keepdims=True)
    o_ref[...] = d * lax.rsqrt(var + LN_EPS) * lng_ref[...] + lnb_ref[...]


def _tail(proj_t, ymo_t, x2, mem_k, mem_vt, ws_t, b_s, wo_t, ln_g, ln_b, batch, seq, alpha):
    tm = TAIL_ROWS
    nt = seq // tm

    def proj_spec(pb):
        return pl.BlockSpec((1, PROJ_BLOCK, tm), lambda i: (i // nt, pb, i % nt))

    return pl.pallas_call(
        functools.partial(_tail_kernel, alpha=alpha),
        grid=(batch * nt,),
        in_specs=[
            proj_spec(PB_U_GM), proj_spec(PB_V_GM), proj_spec(PB_G_GM),
            proj_spec(PB_Q_ME), proj_spec(PB_G_ME),
            pl.BlockSpec((1, MOBA_W, tm), lambda i: (i // nt, 0, i % nt)),
            pl.BlockSpec((tm, D_MODEL), lambda i: (i, 0)),
            pl.BlockSpec((1, MEM_LEN, MEM_W), lambda i: (i // nt, 0, 0)),
            pl.BlockSpec((1, MEM_W, MEM_LEN), lambda i: (i // nt, 0, 0)),
            pl.BlockSpec((GMLP_GROUPS, GMLP_CHUNK, GMLP_CHUNK), lambda i: (0, 0, 0)),
            pl.BlockSpec((GMLP_GROUPS, GMLP_CHUNK), lambda i: (0, 0)),
            pl.BlockSpec((D_MODEL, D_MIX), lambda i: (0, 0)),
            pl.BlockSpec((1, D_MODEL), lambda i: (0, 0)),
            pl.BlockSpec((1, D_MODEL), lambda i: (0, 0)),
        ],
        out_specs=pl.BlockSpec((tm, D_MODEL), lambda i: (i, 0)),
        out_shape=jax.ShapeDtypeStruct((batch * seq, D_MODEL), F32),
        scratch_shapes=[pltpu.VMEM((D_MIX, tm), BF16)],
        compiler_params=pltpu.CompilerParams(
            dimension_semantics=("arbitrary",), vmem_limit_bytes=VMEM_LIMIT),
        name="tail",
    )(proj_t, proj_t, proj_t, proj_t, proj_t, ymo_t, x2, mem_k, mem_vt, ws_t, b_s, wo_t, ln_g, ln_b)


def kernel(x, mem, positions, w_in, w_mem_kv, gmlp_ln_g, gmlp_ln_b, gmlp_w_s, gmlp_b_s,
           w_out, ln_g, ln_b):
    batch, seq, d_model = x.shape
    depth = w_in.shape[0]
    assert d_model == D_MODEL and w_in.shape[2] == D_IN
    assert seq % INPROJ_ROWS == 0 and seq % MOBA_BLOCK == 0 and seq % TAIL_ROWS == 0
    alpha = float((2.0 * depth) ** 0.25)

    half = HEAD_DIM // 2
    inv_freq = ROPE_THETA ** (-jnp.arange(half, dtype=F32) / half)
    inv_freq = jnp.broadcast_to(inv_freq[:, None], (half, LANES))
    pos3 = positions.reshape(batch * seq // INPROJ_ROWS, 1, INPROJ_ROWS)

    x2 = x.reshape(batch * seq, d_model)
    for l in range(depth):
        w_t = w_in[l].T.astype(BF16)
        wk = w_mem_kv[l][:, :MEM_W].astype(BF16)
        wvt = w_mem_kv[l][:, MEM_W:].T.astype(BF16)
        wo_t = w_out[l].T.astype(BF16)
        ws_t = jnp.swapaxes(gmlp_w_s[l], 1, 2)
        lng = jnp.broadcast_to(gmlp_ln_g[l][:, None], (GMLP_W, LANES))
        lnb = jnp.broadcast_to(gmlp_ln_b[l][:, None], (GMLP_W, LANES))

        mem_k, mem_vt = _memkv(mem, wk, wvt)
        proj_t = _inproj(pos3, x2, w_t, inv_freq, lng, lnb, batch, seq)
        ymo_t = _moba(proj_t, batch, seq)
        x2 = _tail(proj_t, ymo_t, x2, mem_k, mem_vt, ws_t, gmlp_b_s[l], wo_t,
                   ln_g[l][None, :], ln_b[l][None, :], batch, seq, alpha)
    return x2.reshape(batch, seq, d_model)
```

```python
import functools

import jax
import jax.numpy as jnp
import numpy as np
from jax import lax
from jax.experimental import pallas as pl
from jax.experimental.pallas import tpu as pltpu

D_MODEL = 2048
HEAD_DIM = 128
MOBA_HEADS = 8
MOBA_W = MOBA_HEADS * HEAD_DIM
MOBA_BLOCK = 256
MOBA_TOPK = 3
GMLP_GROUPS = 4
GMLP_W = GMLP_GROUPS * HEAD_DIM
GMLP_CHUNK = 128
MEM_HEADS = 4
MEM_W = MEM_HEADS * HEAD_DIM
MEM_LEN = 256
D_MIX = MOBA_W + GMLP_W + MEM_W
D_IN = 4 * MOBA_W + 3 * GMLP_W + 2 * MEM_W
ROPE_THETA = 10000.0
LN_EPS = 1e-5
NEG_INF = -1e30
ATTN_SCALE = HEAD_DIM ** -0.5
LOG2E = float(np.log2(np.e))
Q_SCALE = ATTN_SCALE * LOG2E
MOBA_GROUP = 8

LANES = 128
PROJ_BLOCK = 512
N_PROJ_BLOCKS = D_IN // PROJ_BLOCK
PB_Q, PB_K, PB_V, PB_G = 0, 2, 4, 6
PB_U_GM, PB_V_GM, PB_G_GM, PB_Q_ME, PB_G_ME = 8, 9, 10, 11, 12

INPROJ_ROWS = 1024
TAIL_ROWS = 256
VMEM_LIMIT = 48 * 1024 * 1024

F32 = jnp.float32
BF16 = jnp.bfloat16

_NT = (((1,), (1,)), ((), ()))


def _dot(a, b):
    return jnp.dot(a, b, preferred_element_type=F32)


def _dot_nt(a, b):
    return lax.dot_general(a, b, _NT, preferred_element_type=F32)


def _lane_tile(a, reps):
    return a if reps == 1 else jnp.concatenate([a] * reps, axis=1)


def _silu(y):
    return y * (1.0 / (1.0 + jnp.exp(-y)))


def _gelu_tanh(y):
    c = np.float32(np.sqrt(2.0 / np.pi))
    return 0.5 * y * (1.0 + jnp.tanh(c * (y + 0.044715 * (y * y * y))))


def _memkv_kernel(mem_ref, wk_ref, wvt_ref, k_ref, vt_ref):
    m = mem_ref[0].astype(BF16)
    k_ref[0] = _dot(m, wk_ref[...]).astype(BF16)
    vt_ref[0] = _dot_nt(wvt_ref[...], m).astype(BF16)


def _memkv(mem, wk, wvt):
    b = mem.shape[0]
    return pl.pallas_call(
        _memkv_kernel,
        grid=(b,),
        in_specs=[
            pl.BlockSpec((1, MEM_LEN, D_MODEL), lambda i: (i, 0, 0)),
            pl.BlockSpec((D_MODEL, MEM_W), lambda i: (0, 0)),
            pl.BlockSpec((MEM_W, D_MODEL), lambda i: (0, 0)),
        ],
        out_specs=[
            pl.BlockSpec((1, MEM_LEN, MEM_W), lambda i: (i, 0, 0)),
            pl.BlockSpec((1, MEM_W, MEM_LEN), lambda i: (i, 0, 0)),
        ],
        out_shape=[
            jax.ShapeDtypeStruct((b, MEM_LEN, MEM_W), BF16),
            jax.ShapeDtypeStruct((b, MEM_W, MEM_LEN), BF16),
        ],
        compiler_params=pltpu.CompilerParams(
            dimension_semantics=("arbitrary",), vmem_limit_bytes=VMEM_LIMIT),
        name="memkv",
    )(mem, wk, wvt)


def _inproj_kernel(pos_ref, x_ref, w_ref, inv_ref, lng_ref, lnb_ref, o_ref,
                   xb_scr, cos_scr, sin_scr):
    j = pl.program_id(1)
    tm = x_ref.shape[0]
    reps = tm // LANES

    @pl.when(j == 0)
    def _():
        xb_scr[...] = x_ref[...].astype(BF16)
        pos = pos_ref[0].astype(F32)
        ang = _lane_tile(inv_ref[...], reps) * pos
        cos_scr[...] = jnp.cos(ang)
        sin_scr[...] = jnp.sin(ang)

    y = _dot_nt(w_ref[...], xb_scr[...])

    def rope_store(scale):
        half = HEAD_DIM // 2
        c = cos_scr[...]
        s = sin_scr[...]
        for h in range(PROJ_BLOCK // HEAD_DIM):
            x1 = y[h * HEAD_DIM:h * HEAD_DIM + half]
            x2 = y[h * HEAD_DIM + half:(h + 1) * HEAD_DIM]
            o1 = x1 * c - x2 * s
            o2 = x2 * c + x1 * s
            if scale != 1.0:
                o1 = o1 * scale
                o2 = o2 * scale
            o_ref[0, h * HEAD_DIM:h * HEAD_DIM + half, :] = o1.astype(BF16)
            o_ref[0, h * HEAD_DIM + half:(h + 1) * HEAD_DIM, :] = o2.astype(BF16)

    @pl.when(j < PB_K)
    def _():
        rope_store(Q_SCALE)

    @pl.when(jnp.logical_and(j >= PB_K, j < PB_V))
    def _():
        rope_store(1.0)

    @pl.when(jnp.logical_and(j >= PB_V, j < PB_G))
    def _():
        o_ref[0] = y.astype(BF16)

    is_gate = jnp.logical_or(jnp.logical_and(j >= PB_G, j < PB_U_GM),
                             jnp.logical_or(j == PB_G_GM, j == PB_G_ME))

    @pl.when(is_gate)
    def _():
        o_ref[0] = _silu(y).astype(BF16)

    @pl.when(j == PB_U_GM)
    def _():
        o_ref[0] = _gelu_tanh(y).astype(BF16)

    @pl.when(j == PB_V_GM)
    def _():
        v = _gelu_tanh(y)
        mu = jnp.mean(v, axis=0, keepdims=True)
        d = v - mu
        var = jnp.mean(d * d, axis=0, keepdims=True)
        n = d * lax.rsqrt(var + LN_EPS)
        o_ref[0] = (n * _lane_tile(lng_ref[...], reps) + _lane_tile(lnb_ref[...], reps)).astype(BF16)

    @pl.when(j == PB_Q_ME)
    def _():
        o_ref[0] = (y * Q_SCALE).astype(BF16)


def _inproj(pos3, x2, w_t, inv_freq, lng, lnb, batch, seq):
    tm = INPROJ_ROWS
    nt = seq // tm
    return pl.pallas_call(
        _inproj_kernel,
        grid=(batch * nt, N_PROJ_BLOCKS),
        in_specs=[
            pl.BlockSpec((1, 1, tm), lambda i, j: (i, 0, 0)),
            pl.BlockSpec((tm, D_MODEL), lambda i, j: (i, 0)),
            pl.BlockSpec((PROJ_BLOCK, D_MODEL), lambda i, j: (j, 0)),
            pl.BlockSpec((HEAD_DIM // 2, LANES), lambda i, j: (0, 0)),
            pl.BlockSpec((GMLP_W, LANES), lambda i, j: (0, 0)),
            pl.BlockSpec((GMLP_W, LANES), lambda i, j: (0, 0)),
        ],
        out_specs=pl.BlockSpec((1, PROJ_BLOCK, tm), lambda i, j: (i // nt, j, i % nt)),
        out_shape=jax.ShapeDtypeStruct((batch, D_IN, seq), BF16),
        scratch_shapes=[
            pltpu.VMEM((tm, D_MODEL), BF16),
            pltpu.VMEM((HEAD_DIM // 2, tm), F32),
            pltpu.VMEM((HEAD_DIM // 2, tm), F32),
        ],
        compiler_params=pltpu.CompilerParams(
            dimension_semantics=("arbitrary", "arbitrary"), vmem_limit_bytes=VMEM_LIMIT),
        name="inproj",
    )(pos3, x2, w_t, inv_freq, lng, lnb)


def _moba_kernel(q_ref, kt_ref, vt_ref, g_ref, o_ref, k_scr, km_scr, qa_scr, sa_scr, sb_scr):
    i = pl.program_id(2)
    nb = kt_ref.shape[2] // MOBA_BLOCK
    blk = MOBA_BLOCK

    @pl.when(i == 0)
    def _():
        col = lax.broadcasted_iota(jnp.int32, (blk, LANES), 1)

        def body(c, carry):
            start = pl.multiple_of(c * blk, blk)
            k = kt_ref[0, :, pl.ds(start, blk)].astype(F32).T
            k_scr[pl.ds(start, blk), 0:HEAD_DIM] = k.astype(BF16)
            k_scr[pl.ds(start, blk), HEAD_DIM:2 * HEAD_DIM] = jnp.where(col == c, 1.0, 0.0).astype(BF16)
            km_scr[pl.ds(c, 1), :] = jnp.sum(k, axis=0, keepdims=True) * (1.0 / blk)
            return carry

        lax.fori_loop(0, nb, body, 0)

    qt = q_ref[0]
    km = km_scr[...]
    km_hi = km.astype(BF16)
    km_lo = (km - km_hi.astype(F32)).astype(BF16)
    gate = _dot(km_hi, qt) + _dot(km_lo, qt)
    bid = lax.broadcasted_iota(jnp.int32, gate.shape, 0)
    past = bid < i
    g = jnp.where(past, gate, NEG_INF)
    sel = jnp.zeros(gate.shape, F32)
    for _ in range(MOBA_TOPK):
        mx = jnp.max(g, axis=0, keepdims=True)
        first = jnp.min(jnp.where(g == mx, bid, nb), axis=0, keepdims=True)
        hit = bid == first
        sel = jnp.where(jnp.logical_and(hit, past), 1.0, sel)
        g = jnp.where(hit, -jnp.inf, g)
    bias = jnp.where(jnp.logical_or(sel > 0.0, bid == i), 0.0, NEG_INF)
    qa_scr[0:HEAD_DIM, :] = qt
    qa_scr[HEAD_DIM:HEAD_DIM + nb, :] = bias.astype(BF16)
    if HEAD_DIM + nb < 2 * HEAD_DIM:
        qa_scr[HEAD_DIM + nb:2 * HEAD_DIM, :] = jnp.zeros((HEAD_DIM - nb, blk), BF16)

    gk = MOBA_GROUP * blk
    n_groups = nb // MOBA_GROUP
    n = i // MOBA_GROUP

    def group_at(t):
        return jnp.where(t == 0, n, jnp.where(t <= n, t - 1, jnp.minimum(t, n_groups - 1)))

    def score(g, s_ref, causal):
        start = pl.multiple_of(g * gk, gk)
        half = gk // 2
        qa = qa_scr[...]
        s = jnp.concatenate([_dot(k_scr[pl.ds(start, half), :], qa),
                             _dot(k_scr[pl.ds(start + half, half), :], qa)], axis=0)
        if causal:
            kk = start + lax.broadcasted_iota(jnp.int32, s.shape, 0)
            qq = i * blk + lax.broadcasted_iota(jnp.int32, s.shape, 1)
            s = jnp.where(kk <= qq, s, NEG_INF)
        s_ref[...] = s
        return jnp.max(s, axis=0, keepdims=True)

    def absorb(g, s_ref, mx, m, l, acc):
        start = pl.multiple_of(g * gk, gk)
        m_new = jnp.maximum(m, mx)
        alpha = jnp.exp2(m - m_new)
        p = jnp.exp2(s_ref[...] - m_new)
        l = alpha * l + jnp.sum(p, axis=0, keepdims=True)
        acc = alpha * acc + _dot(vt_ref[0, :, pl.ds(start, gk)], p.astype(BF16))
        return m_new, l, acc

    def pair(u, carry):
        m, l, acc, mx_a = carry
        g0, g1, g2 = group_at(2 * u), group_at(2 * u + 1), group_at(2 * u + 2)
        mx_b = score(g1, sb_scr, False)
        m, l, acc = absorb(g0, sa_scr, mx_a, m, l, acc)
        mx_a = score(g2, sa_scr, False)
        m, l, acc = absorb(g1, sb_scr, mx_b, m, l, acc)
        return m, l, acc, mx_a

    init = (jnp.full((1, blk), NEG_INF, F32), jnp.zeros((1, blk), F32),
            jnp.zeros((HEAD_DIM, blk), F32), score(n, sa_scr, True))
    _, l, acc, _ = lax.fori_loop(0, (n + 2) // 2, pair, init)
    o_ref[0] = (acc / l * g_ref[0].astype(F32)).astype(BF16)


def _moba(proj_t, batch, seq):
    nq = seq // MOBA_BLOCK
    nb = nq
    assert nb <= HEAD_DIM, "block mask rows must fit beside the head dim in one 256-deep contraction"
    assert nb % (2 * MOBA_GROUP) == 0, "the group pipeline pads odd group counts with a future group"
    hb = PROJ_BLOCK // HEAD_DIM
    return pl.pallas_call(
        _moba_kernel,
        grid=(batch, MOBA_HEADS, nq),
        in_specs=[
            pl.BlockSpec((1, HEAD_DIM, MOBA_BLOCK), lambda b, h, i: (b, PB_Q * hb + h, i)),
            pl.BlockSpec((1, HEAD_DIM, seq), lambda b, h, i: (b, PB_K * hb + h, 0)),
            pl.BlockSpec((1, HEAD_DIM, seq), lambda b, h, i: (b, PB_V * hb + h, 0)),
            pl.BlockSpec((1, HEAD_DIM, MOBA_BLOCK), lambda b, h, i: (b, PB_G * hb + h, i)),
        ],
        out_specs=pl.BlockSpec((1, HEAD_DIM, MOBA_BLOCK), lambda b, h, i: (b, h, i)),
        out_shape=jax.ShapeDtypeStruct((batch, MOBA_W, seq), BF16),
        scratch_shapes=[
            pltpu.VMEM((seq, 2 * HEAD_DIM), BF16),
            pltpu.VMEM((nb, HEAD_DIM), F32),
            pltpu.VMEM((2 * HEAD_DIM, MOBA_BLOCK), BF16),
            pltpu.VMEM((MOBA_GROUP * MOBA_BLOCK, MOBA_BLOCK), F32),
            pltpu.VMEM((MOBA_GROUP * MOBA_BLOCK, MOBA_BLOCK), F32),
        ],
        compiler_params=pltpu.CompilerParams(
            dimension_semantics=("arbitrary", "arbitrary", "arbitrary"),
            vmem_limit_bytes=VMEM_LIMIT),
        name="moba",
    )(proj_t, proj_t, proj_t, proj_t)


def _tail_kernel(u_ref, v_ref, gg_ref, qm_ref, gm_ref, ymo_ref, x_ref, mk_ref, mvt_ref,
                 wst_ref, bs_ref, wot_ref, lng_ref, lnb_ref, o_ref, y_scr, *, alpha):
    tm = x_ref.shape[0]
    nchunk = tm // GMLP_CHUNK
    hd = HEAD_DIM

    y_scr[0:MOBA_W, :] = ymo_ref[0]

    rr = lax.broadcasted_iota(jnp.int32, (GMLP_CHUNK, GMLP_CHUNK), 0)
    cc = lax.broadcasted_iota(jnp.int32, (GMLP_CHUNK, GMLP_CHUNK), 1)
    for g in range(GMLP_GROUPS):
        rows = slice(g * hd, (g + 1) * hd)
        w_t = jnp.where(rr <= cc, wst_ref[g], 0.0).astype(BF16)
        vg = v_ref[0, rows, :]
        stacked = jnp.concatenate(
            [vg[:, n * GMLP_CHUNK:(n + 1) * GMLP_CHUNK] for n in range(nchunk)], axis=0)
        mixed = _dot(stacked, w_t)
        mixed_t = jnp.concatenate(
            [mixed[n * hd:(n + 1) * hd, :] for n in range(nchunk)], axis=1)
        mixed_t = mixed_t + _lane_tile(bs_ref[g:g + 1, :], nchunk)
        yg = u_ref[0, rows, :].astype(F32) * mixed_t * gg_ref[0, rows, :].astype(F32)
        y_scr[MOBA_W + g * hd:MOBA_W + (g + 1) * hd, :] = yg.astype(BF16)

    for h in range(MEM_HEADS):
        rows = slice(h * hd, (h + 1) * hd)
        s = _dot(mk_ref[0, :, rows], qm_ref[0, rows, :])
        m = jnp.max(s, axis=0, keepdims=True)
        p = jnp.exp2(s - m)
        l = jnp.sum(p, axis=0, keepdims=True)
        o = _dot(mvt_ref[0, rows, :], p.astype(BF16))
        ym = o / l * gm_ref[0, rows, :].astype(F32)
        y_scr[MOBA_W + GMLP_W + h * hd:MOBA_W + GMLP_W + (h + 1) * hd, :] = ym.astype(BF16)

    sub = _dot(wot_ref[...], y_scr[...]).T
    z = alpha * x_ref[...] + sub
    mu = jnp.mean(z, axis=-1, keepdims=True)
    d = z - mu
    var = jnp.mean(d * d, axis=-1, keepdims=True)
    o_ref[...] = d * lax.rsqrt(var + LN_EPS) * lng_ref[...] + lnb_ref[...]


def _tail(proj_t, ymo_t, x2, mem_k, mem_vt, ws_t, b_s, wo_t, ln_g, ln_b, batch, seq, alpha):
    tm = TAIL_ROWS
    nt = seq // tm

    def proj_spec(pb):
        return pl.BlockSpec((1, PROJ_BLOCK, tm), lambda i: (i // nt, pb, i % nt))

    return pl.pallas_call(
        functools.partial(_tail_kernel, alpha=alpha),
        grid=(batch * nt,),
        in_specs=[
            proj_spec(PB_U_GM), proj_spec(PB_V_GM), proj_spec(PB_G_GM),
            proj_spec(PB_Q_ME), proj_spec(PB_G_ME),
            pl.BlockSpec((1, MOBA_W, tm), lambda i: (i // nt, 0, i % nt)),
            pl.BlockSpec((tm, D_MODEL), lambda i: (i, 0)),
            pl.BlockSpec((1, MEM_LEN, MEM_W), lambda i: (i // nt, 0, 0)),
            pl.BlockSpec((1, MEM_W, MEM_LEN), lambda i: (i // nt, 0, 0)),
            pl.BlockSpec((GMLP_GROUPS, GMLP_CHUNK, GMLP_CHUNK), lambda i: (0, 0, 0)),
            pl.BlockSpec((GMLP_GROUPS, GMLP_CHUNK), lambda i: (0, 0)),
            pl.BlockSpec((D_MODEL, D_MIX), lambda i: (0, 0)),
            pl.BlockSpec((1, D_MODEL), lambda i: (0, 0)),
            pl.BlockSpec((1, D_MODEL), lambda i: (0, 0)),
        ],
        out_specs=pl.BlockSpec((tm, D_MODEL), lambda i: (i, 0)),
        out_shape=jax.ShapeDtypeStruct((batch * seq, D_MODEL), F32),
        scratch_shapes=[pltpu.VMEM((D_MIX, tm), BF16)],
        compiler_params=pltpu.CompilerParams(
            dimension_semantics=("arbitrary",), vmem_limit_bytes=VMEM_LIMIT),
        name="tail",
    )(proj_t, proj_t, proj_t, proj_t, proj_t, ymo_t, x2, mem_k, mem_vt, ws_t, b_s, wo_t, ln_g, ln_b)


def kernel(x, mem, positions, w_in, w_mem_kv, gmlp_ln_g, gmlp_ln_b, gmlp_w_s, gmlp_b_s,
           w_out, ln_g, ln_b):
    batch, seq, d_model = x.shape
    depth = w_in.shape[0]
    assert d_model == D_MODEL and w_in.shape[2] == D_IN
    assert seq % INPROJ_ROWS == 0 and seq % MOBA_BLOCK == 0 and seq % TAIL_ROWS == 0
    alpha = float((2.0 * depth) ** 0.25)

    half = HEAD_DIM // 2
    inv_freq = ROPE_THETA ** (-jnp.arange(half, dtype=F32) / half)
    inv_freq = jnp.broadcast_to(inv_freq[:, None], (half, LANES))
    pos3 = positions.reshape(batch * seq // INPROJ_ROWS, 1, INPROJ_ROWS)

    x2 = x.reshape(batch * seq, d_model)
    for l in range(depth):
        w_t = w_in[l].T.astype(BF16)
        wk = w_mem_kv[l][:, :MEM_W].astype(BF16)
        wvt = w_mem_kv[l][:, MEM_W:].T.astype(BF16)
        wo_t = w_out[l].T.astype(BF16)
        ws_t = jnp.swapaxes(gmlp_w_s[l], 1, 2)
        lng = jnp.broadcast_to(gmlp_ln_g[l][:, None], (GMLP_W, LANES))
        lnb = jnp.broadcast_to(gmlp_ln_b[l][:, None], (GMLP_W, LANES))

        mem_k, mem_vt = _memkv(mem, wk, wvt)
        proj_t = _inproj(pos3, x2, w_t, inv_freq, lng, lnb, batch, seq)
        ymo_t = _moba(proj_t, batch, seq)
        x2 = _tail(proj_t, ymo_t, x2, mem_k, mem_vt, ws_t, gmlp_b_s[l], wo_t,
                   ln_g[l][None, :], ln_b[l][None, :], batch, seq, alpha)
    return x2.reshape(batch, seq, d_model)
```

```python
import functools

import jax
import jax.numpy as jnp
import numpy as np
from jax import lax
from jax.experimental import pallas as pl
from jax.experimental.pallas import tpu as pltpu

D_MODEL = 2048
HEAD_DIM = 128
MOBA_HEADS = 8
MOBA_W = MOBA_HEADS * HEAD_DIM
MOBA_BLOCK = 256
MOBA_TOPK = 3
GMLP_GROUPS = 4
GMLP_W = GMLP_GROUPS * HEAD_DIM
GMLP_CHUNK = 128
MEM_HEADS = 4
MEM_W = MEM_HEADS * HEAD_DIM
MEM_LEN = 256
D_MIX = MOBA_W + GMLP_W + MEM_W
D_IN = 4 * MOBA_W + 3 * GMLP_W + 2 * MEM_W
ROPE_THETA = 10000.0
LN_EPS = 1e-5
NEG_INF = -1e30
ATTN_SCALE = HEAD_DIM ** -0.5
LOG2E = float(np.log2(np.e))
Q_SCALE = ATTN_SCALE * LOG2E
MOBA_GROUP = 8
ONES_ROWS = 16

LANES = 128
PROJ_BLOCK = 512
N_PROJ_BLOCKS = D_IN // PROJ_BLOCK
PB_Q, PB_K, PB_V, PB_G = 0, 2, 4, 6
PB_U_GM, PB_V_GM, PB_G_GM, PB_Q_ME, PB_G_ME = 8, 9, 10, 11, 12

INPROJ_ROWS = 1024
TAIL_ROWS = 256
VMEM_LIMIT = 48 * 1024 * 1024
MOBA_VMEM_LIMIT = 56 * 1024 * 1024

F32 = jnp.float32
BF16 = jnp.bfloat16

_NT = (((1,), (1,)), ((), ()))


def _dot(a, b):
    return jnp.dot(a, b, preferred_element_type=F32)


def _dot_nt(a, b):
    return lax.dot_general(a, b, _NT, preferred_element_type=F32)


def _lane_tile(a, reps):
    return a if reps == 1 else jnp.concatenate([a] * reps, axis=1)


def _silu(y):
    return y * (1.0 / (1.0 + jnp.exp(-y)))


def _gelu_tanh(y):
    c = np.float32(np.sqrt(2.0 / np.pi))
    return 0.5 * y * (1.0 + jnp.tanh(c * (y + 0.044715 * (y * y * y))))


def _memkv_kernel(mem_ref, wk_ref, wvt_ref, k_ref, vt_ref):
    m = mem_ref[0].astype(BF16)
    k_ref[0] = _dot(m, wk_ref[...]).astype(BF16)
    vt_ref[0] = _dot_nt(wvt_ref[...], m).astype(BF16)


def _memkv(mem, wk, wvt):
    b = mem.shape[0]
    return pl.pallas_call(
        _memkv_kernel,
        grid=(b,),
        in_specs=[
            pl.BlockSpec((1, MEM_LEN, D_MODEL), lambda i: (i, 0, 0)),
            pl.BlockSpec((D_MODEL, MEM_W), lambda i: (0, 0)),
            pl.BlockSpec((MEM_W, D_MODEL), lambda i: (0, 0)),
        ],
        out_specs=[
            pl.BlockSpec((1, MEM_LEN, MEM_W), lambda i: (i, 0, 0)),
            pl.BlockSpec((1, MEM_W, MEM_LEN), lambda i: (i, 0, 0)),
        ],
        out_shape=[
            jax.ShapeDtypeStruct((b, MEM_LEN, MEM_W), BF16),
            jax.ShapeDtypeStruct((b, MEM_W, MEM_LEN), BF16),
        ],
        compiler_params=pltpu.CompilerParams(
            dimension_semantics=("arbitrary",), vmem_limit_bytes=VMEM_LIMIT),
        name="memkv",
    )(mem, wk, wvt)


def _inproj_kernel(pos_ref, x_ref, w_ref, inv_ref, lng_ref, lnb_ref, o_ref,
                   xb_scr, cos_scr, sin_scr):
    j = pl.program_id(1)
    tm = x_ref.shape[0]
    reps = tm // LANES

    @pl.when(j == 0)
    def _():
        xb_scr[...] = x_ref[...].astype(BF16)
        pos = pos_ref[0].astype(F32)
        ang = _lane_tile(inv_ref[...], reps) * pos
        cos_scr[...] = jnp.cos(ang)
        sin_scr[...] = jnp.sin(ang)

    y = _dot_nt(w_ref[...], xb_scr[...])

    def rope_store(scale):
        half = HEAD_DIM // 2
        c = cos_scr[...]
        s = sin_scr[...]
        for h in range(PROJ_BLOCK // HEAD_DIM):
            x1 = y[h * HEAD_DIM:h * HEAD_DIM + half]
            x2 = y[h * HEAD_DIM + half:(h + 1) * HEAD_DIM]
            o1 = x1 * c - x2 * s
            o2 = x2 * c + x1 * s
            if scale != 1.0:
                o1 = o1 * scale
                o2 = o2 * scale
            o_ref[0, h * HEAD_DIM:h * HEAD_DIM + half, :] = o1.astype(BF16)
            o_ref[0, h * HEAD_DIM + half:(h + 1) * HEAD_DIM, :] = o2.astype(BF16)

    @pl.when(j < PB_K)
    def _():
        rope_store(Q_SCALE)

    @pl.when(jnp.logical_and(j >= PB_K, j < PB_V))
    def _():
        rope_store(1.0)

    @pl.when(jnp.logical_and(j >= PB_V, j < PB_G))
    def _():
        o_ref[0] = y.astype(BF16)

    is_gate = jnp.logical_or(jnp.logical_and(j >= PB_G, j < PB_U_GM),
                             jnp.logical_or(j == PB_G_GM, j == PB_G_ME))

    @pl.when(is_gate)
    def _():
        o_ref[0] = _silu(y).astype(BF16)

    @pl.when(j == PB_U_GM)
    def _():
        o_ref[0] = _gelu_tanh(y).astype(BF16)

    @pl.when(j == PB_V_GM)
    def _():
        v = _gelu_tanh(y)
        mu = jnp.mean(v, axis=0, keepdims=True)
        d = v - mu
        var = jnp.mean(d * d, axis=0, keepdims=True)
        n = d * lax.rsqrt(var + LN_EPS)
        o_ref[0] = (n * _lane_tile(lng_ref[...], reps) + _lane_tile(lnb_ref[...], reps)).astype(BF16)

    @pl.when(j == PB_Q_ME)
    def _():
        o_ref[0] = (y * Q_SCALE).astype(BF16)


def _inproj(pos3, x2, w_t, inv_freq, lng, lnb, batch, seq):
    tm = INPROJ_ROWS
    nt = seq // tm
    return pl.pallas_call(
        _inproj_kernel,
        grid=(batch * nt, N_PROJ_BLOCKS),
        in_specs=[
            pl.BlockSpec((1, 1, tm), lambda i, j: (i, 0, 0)),
            pl.BlockSpec((tm, D_MODEL), lambda i, j: (i, 0)),
            pl.BlockSpec((PROJ_BLOCK, D_MODEL), lambda i, j: (j, 0)),
            pl.BlockSpec((HEAD_DIM // 2, LANES), lambda i, j: (0, 0)),
            pl.BlockSpec((GMLP_W, LANES), lambda i, j: (0, 0)),
            pl.BlockSpec((GMLP_W, LANES), lambda i, j: (0, 0)),
        ],
        out_specs=pl.BlockSpec((1, PROJ_BLOCK, tm), lambda i, j: (i // nt, j, i % nt)),
        out_shape=jax.ShapeDtypeStruct((batch, D_IN, seq), BF16),
        scratch_shapes=[
            pltpu.VMEM((tm, D_MODEL), BF16),
            pltpu.VMEM((HEAD_DIM // 2, tm), F32),
            pltpu.VMEM((HEAD_DIM // 2, tm), F32),
        ],
        compiler_params=pltpu.CompilerParams(
            dimension_semantics=("arbitrary", "arbitrary"), vmem_limit_bytes=VMEM_LIMIT),
        name="inproj",
    )(pos3, x2, w_t, inv_freq, lng, lnb)


def _moba_kernel(q_ref, kt_ref, vt_ref, g_ref, o_ref, k_scr, km_scr, bias_scr, va_scr,
                 sa_scr, sb_scr, pa_scr, pb_scr, *, n_steps):
    seq = kt_ref.shape[2]
    blk = MOBA_BLOCK
    nb = seq // blk
    gk = MOBA_GROUP * blk

    col = lax.broadcasted_iota(jnp.int32, (blk, LANES), 1)

    def k_body(c, carry):
        start = pl.multiple_of(c * blk, blk)
        k = kt_ref[0, :, pl.ds(start, blk)].astype(F32).T
        k_scr[pl.ds(start, blk), 0:HEAD_DIM] = k.astype(BF16)
        k_scr[pl.ds(start, blk), HEAD_DIM:2 * HEAD_DIM] = jnp.where(col == c, 1.0, 0.0).astype(BF16)
        km_scr[pl.ds(c, 1), :] = jnp.sum(k, axis=0, keepdims=True) * (1.0 / blk)
        return carry

    lax.fori_loop(0, nb, k_body, 0)
    va_scr[0:HEAD_DIM, :] = vt_ref[0]
    va_scr[HEAD_DIM:HEAD_DIM + ONES_ROWS, :] = jnp.ones((ONES_ROWS, seq), BF16)

    km = km_scr[...]
    km_hi = km.astype(BF16)
    km_lo = (km - km_hi.astype(F32)).astype(BF16)
    bid = lax.broadcasted_iota(jnp.int32, (nb, blk), 0)

    def gate_body(c, carry):
        cols = pl.ds(pl.multiple_of(c * blk, blk), blk)
        qt = q_ref[0, :, cols]
        gate = _dot(km_hi, qt) + _dot(km_lo, qt)
        past = bid < c
        g = jnp.where(past, gate, NEG_INF)
        sel = jnp.zeros(gate.shape, F32)
        for _ in range(MOBA_TOPK):
            mx = jnp.max(g, axis=0, keepdims=True)
            first = jnp.min(jnp.where(g == mx, bid, nb), axis=0, keepdims=True)
            hit = bid == first
            sel = jnp.where(jnp.logical_and(hit, past), 1.0, sel)
            g = jnp.where(hit, -jnp.inf, g)
        bias = jnp.where(jnp.logical_or(sel > 0.0, bid == c), 0.0, NEG_INF)
        bias_scr[:, cols] = bias.astype(BF16)
        return carry

    lax.fori_loop(0, nb, gate_body, 0)

    def next_step(tile, pos):
        last = pos >= tile // MOBA_GROUP
        return (jnp.where(last, jnp.minimum(tile + 1, nb - 1), tile),
                jnp.where(last, 0, pos + 1))

    def group_start(tile, pos):
        g = jnp.where(pos == 0, tile // MOBA_GROUP, pos - 1)
        return pl.multiple_of(g * gk, gk)

    def tile_cols(tile):
        return pl.ds(pl.multiple_of(tile * blk, blk), blk)

    def score(step, s_ref):
        tile, pos = step
        cols = tile_cols(tile)
        parts = [q_ref[0, :, cols], bias_scr[:, cols]]
        if nb < HEAD_DIM:
            parts.append(jnp.zeros((HEAD_DIM - nb, blk), BF16))
        qa = jnp.concatenate(parts, axis=0)
        start = group_start(tile, pos)
        half = gk // 2
        s_ref[0:half, :] = _dot(k_scr[pl.ds(start, half), :], qa)
        s_ref[half:gk, :] = _dot(k_scr[pl.ds(start + half, half), :], qa)

    kk = lax.broadcasted_iota(jnp.int32, (blk, blk), 0)
    qq = lax.broadcasted_iota(jnp.int32, (blk, blk), 1)

    def soften(step, s_ref, p_ref, m):
        tile, pos = step
        own = pl.ds(pl.multiple_of((tile % MOBA_GROUP) * blk, blk), blk)
        keep = kk <= qq + jnp.where(pos == 0, 0, blk)
        s_ref[own, :] = jnp.where(keep, s_ref[own, :], NEG_INF)
        s = s_ref[...]
        m_prev = jnp.minimum(m, jnp.where(pos == 0, NEG_INF, -NEG_INF).astype(F32))
        m_new = jnp.maximum(m_prev, jnp.max(s, axis=0, keepdims=True))
        p_ref[...] = jnp.exp2((s - m_new).astype(BF16))
        return m_new, jnp.exp2(m_prev - m_new)

    def absorb(step, p_ref, factor, acc):
        tile, pos = step
        start = group_start(tile, pos)
        acc = factor * acc + _dot(va_scr[:, pl.ds(start, gk)], p_ref[...])
        cols = tile_cols(tile)
        l = acc[HEAD_DIM:HEAD_DIM + 1, :]
        o_ref[0, :, cols] = (acc[0:HEAD_DIM, :] / l * g_ref[0, :, cols].astype(F32)).astype(BF16)
        return acc

    zero = jnp.int32(0)
    t0 = (zero, zero)
    t1 = next_step(*t0)
    t2 = next_step(*t1)
    score(t0, sa_scr)
    m, f0 = soften(t0, sa_scr, pa_scr, jnp.full((1, blk), NEG_INF, F32))
    score(t1, sb_scr)

    def pair(u, carry):
        a0, a1, b0, b1, c0, c1, m, f, acc = carry
        ta, tb, tc = (a0, a1), (b0, b1), (c0, c1)
        td = next_step(*tc)
        te = next_step(*td)
        score(tc, sa_scr)
        m, fb = soften(tb, sb_scr, pb_scr, m)
        acc = absorb(ta, pa_scr, f, acc)
        score(td, sb_scr)
        m, fc = soften(tc, sa_scr, pa_scr, m)
        acc = absorb(tb, pb_scr, fb, acc)
        return tc + td + te + (m, fc, acc)

    init = t0 + t1 + t2 + (m, f0, jnp.zeros((HEAD_DIM + ONES_ROWS, blk), F32))
    lax.fori_loop(0, n_steps // 2, pair, init)


def _moba(proj_t, batch, seq):
    nb = seq // MOBA_BLOCK
    assert nb <= HEAD_DIM and nb % 16 == 0, "mask rows sit beside the head dim in one 256-deep contraction"
    n_steps = sum(i // MOBA_GROUP + 1 for i in range(nb))
    assert n_steps % 2 == 0 and nb % MOBA_GROUP == 0
    hb = PROJ_BLOCK // HEAD_DIM

    def head_spec(pb):
        return pl.BlockSpec((1, HEAD_DIM, seq), lambda b, h: (b, pb * hb + h, 0),
                            pipeline_mode=pl.Buffered(1))

    gk = MOBA_GROUP * MOBA_BLOCK
    return pl.pallas_call(
        functools.partial(_moba_kernel, n_steps=n_steps),
        grid=(batch, MOBA_HEADS),
        in_specs=[head_spec(PB_Q), head_spec(PB_K), head_spec(PB_V), head_spec(PB_G)],
        out_specs=pl.BlockSpec((1, HEAD_DIM, seq), lambda b, h: (b, h, 0)),
        out_shape=jax.ShapeDtypeStruct((batch, MOBA_W, seq), BF16),
        scratch_shapes=[
            pltpu.VMEM((seq, 2 * HEAD_DIM), BF16),
            pltpu.VMEM((nb, HEAD_DIM), F32),
            pltpu.VMEM((nb, seq), BF16),
            pltpu.VMEM((HEAD_DIM + ONES_ROWS, seq), BF16),
            pltpu.VMEM((gk, MOBA_BLOCK), F32),
            pltpu.VMEM((gk, MOBA_BLOCK), F32),
            pltpu.VMEM((gk, MOBA_BLOCK), BF16),
            pltpu.VMEM((gk, MOBA_BLOCK), BF16),
        ],
        compiler_params=pltpu.CompilerParams(
            dimension_semantics=("arbitrary", "arbitrary"),
            vmem_limit_bytes=MOBA_VMEM_LIMIT),
        name="moba",
    )(proj_t, proj_t, proj_t, proj_t)


def _tail_kernel(u_ref, v_ref, gg_ref, qm_ref, gm_ref, ymo_ref, x_ref, mk_ref, mvt_ref,
                 wst_ref, bs_ref, wot_ref, lng_ref, lnb_ref, o_ref, y_scr, *, alpha):
    tm = x_ref.shape[0]
    nchunk = tm // GMLP_CHUNK
    hd = HEAD_DIM

    y_scr[0:MOBA_W, :] = ymo_ref[0]

    rr = lax.broadcasted_iota(jnp.int32, (GMLP_CHUNK, GMLP_CHUNK), 0)
    cc = lax.broadcasted_iota(jnp.int32, (GMLP_CHUNK, GMLP_CHUNK), 1)
    for g in range(GMLP_GROUPS):
        rows = slice(g * hd, (g + 1) * hd)
        w_t = jnp.where(rr <= cc, wst_ref[g], 0.0).astype(BF16)
        vg = v_ref[0, rows, :]
        stacked = jnp.concatenate(
            [vg[:, n * GMLP_CHUNK:(n + 1) * GMLP_CHUNK] for n in range(nchunk)], axis=0)
        mixed = _dot(stacked, w_t)
        mixed_t = jnp.concatenate(
            [mixed[n * hd:(n + 1) * hd, :] for n in range(nchunk)], axis=1)
        mixed_t = mixed_t + _lane_tile(bs_ref[g:g + 1, :], nchunk)
        yg = u_ref[0, rows, :].astype(F32) * mixed_t * gg_ref[0, rows, :].astype(F32)
        y_scr[MOBA_W + g * hd:MOBA_W + (g + 1) * hd, :] = yg.astype(BF16)

    for h in range(MEM_HEADS):
        rows = slice(h * hd, (h + 1) * hd)
        s = _dot(mk_ref[0, :, rows], qm_ref[0, rows, :])
        m = jnp.max(s, axis=0, keepdims=True)
        p = jnp.exp2(s - m)
        l = jnp.sum(p, axis=0, keepdims=True)
        o = _dot(mvt_ref[0, rows, :], p.astype(BF16))
        ym = o / l * gm_ref[0, rows, :].astype(F32)
        y_scr[MOBA_W + GMLP_W + h * hd:MOBA_W + GMLP_W + (h + 1) * hd, :] = ym.astype(BF16)

    sub = _dot(wot_ref[...], y_scr[...]).T
    z = alpha * x_ref[...] + sub
    mu = jnp.mean(z, axis=-1, keepdims=True)
    d = z - mu
    var = jnp.mean(d * d, axis=-1, keepdims=True)
    o_ref[...] = d * lax.rsqrt(var + LN_EPS) * lng_ref[...] + lnb_ref[...]


def _tail(proj_t, ymo_t, x2, mem_k, mem_vt, ws_t, b_s, wo_t, ln_g, ln_b, batch, seq, alpha):
    tm = TAIL_ROWS
    nt = seq // tm

    def proj_spec(pb):
        return pl.BlockSpec((1, PROJ_BLOCK, tm), lambda i: (i // nt, pb, i % nt))

    return pl.pallas_call(
        functools.partial(_tail_kernel, alpha=alpha),
        grid=(batch * nt,),
        in_specs=[
            proj_spec(PB_U_GM), proj_spec(PB_V_GM), proj_spec(PB_G_GM),
            proj_spec(PB_Q_ME), proj_spec(PB_G_ME),
            pl.BlockSpec((1, MOBA_W, tm), lambda i: (i // nt, 0, i % nt)),
            pl.BlockSpec((tm, D_MODEL), lambda i: (i, 0)),
            pl.BlockSpec((1, MEM_LEN, MEM_W), lambda i: (i // nt, 0, 0)),
            pl.BlockSpec((1, MEM_W, MEM_LEN), lambda i: (i // nt, 0, 0)),
            pl.BlockSpec((GMLP_GROUPS, GMLP_CHUNK, GMLP_CHUNK), lambda i: (0, 0, 0)),
            pl.BlockSpec((GMLP_GROUPS, GMLP_CHUNK), lambda i: (0, 0)),
            pl.BlockSpec((D_MODEL, D_MIX), lambda i: (0, 0)),
            pl.BlockSpec((1, D_MODEL), lambda i: (0, 0)),
            pl.BlockSpec((1, D_MODEL), lambda i: (0, 0)),
        ],
        out_specs=pl.BlockSpec((tm, D_MODEL), lambda i: (i, 0)),
        out_shape=jax.ShapeDtypeStruct((batch * seq, D_MODEL), F32),
        scratch_shapes=[pltpu.VMEM((D_MIX, tm), BF16)],
        compiler_params=pltpu.CompilerParams(
            dimension_semantics=("arbitrary",), vmem_limit_bytes=VMEM_LIMIT),
        name="tail",
    )(proj_t, proj_t, proj_t, proj_t, proj_t, ymo_t, x2, mem_k, mem_vt, ws_t, b_s, wo_t, ln_g, ln_b)


def kernel(x, mem, positions, w_in, w_mem_kv, gmlp_ln_g, gmlp_ln_b, gmlp_w_s, gmlp_b_s,
           w_out, ln_g, ln_b):
    batch, seq, d_model = x.shape
    depth = w_in.shape[0]
    assert d_model == D_MODEL and w_in.shape[2] == D_IN
    assert seq % INPROJ_ROWS == 0 and seq % MOBA_BLOCK == 0 and seq % TAIL_ROWS == 0
    alpha = float((2.0 * depth) ** 0.25)

    half = HEAD_DIM // 2
    inv_freq = ROPE_THETA ** (-jnp.arange(half, dtype=F32) / half)
    inv_freq = jnp.broadcast_to(inv_freq[:, None], (half, LANES))
    pos3 = positions.reshape(batch * seq // INPROJ_ROWS, 1, INPROJ_ROWS)

    x2 = x.reshape(batch * seq, d_model)
    for l in range(depth):
        w_t = w_in[l].T.astype(BF16)
        wk = w_mem_kv[l][:, :MEM_W].astype(BF16)
        wvt = w_mem_kv[l][:, MEM_W:].T.astype(BF16)
        wo_t = w_out[l].T.astype(BF16)
        ws_t = jnp.swapaxes(gmlp_w_s[l], 1, 2)
        lng = jnp.broadcast_to(gmlp_ln_g[l][:, None], (GMLP_W, LANES))
        lnb = jnp.broadcast_to(gmlp_ln_b[l][:, None], (GMLP_W, LANES))

        mem_k, mem_vt = _memkv(mem, wk, wvt)
        proj_t = _inproj(pos3, x2, w_t, inv_freq, lng, lnb, batch, seq)
        ymo_t = _moba(proj_t, batch, seq)
        x2 = _tail(proj_t, ymo_t, x2, mem_k, mem_vt, ws_t, gmlp_b_s[l], wo_t,
                   ln_g[l][None, :], ln_b[l][None, :], batch, seq, alpha)
    return x2.reshape(batch, seq, d_model)
```

```python
import functools

import jax
import jax.numpy as jnp
import numpy as np
from jax import lax
from jax.experimental import pallas as pl
from jax.experimental.pallas import tpu as pltpu

D_MODEL = 2048
HEAD_DIM = 128
MOBA_HEADS = 8
MOBA_W = MOBA_HEADS * HEAD_DIM
MOBA_BLOCK = 256
MOBA_TOPK = 3
GMLP_GROUPS = 4
GMLP_W = GMLP_GROUPS * HEAD_DIM
GMLP_CHUNK = 128
MEM_HEADS = 4
MEM_W = MEM_HEADS * HEAD_DIM
MEM_LEN = 256
D_MIX = MOBA_W + GMLP_W + MEM_W
D_IN = 4 * MOBA_W + 3 * GMLP_W + 2 * MEM_W
ROPE_THETA = 10000.0
LN_EPS = 1e-5
NEG_INF = -1e30
ATTN_SCALE = HEAD_DIM ** -0.5
LOG2E = float(np.log2(np.e))
Q_SCALE = ATTN_SCALE * LOG2E
MOBA_GROUP = 8
MOBA_UNROLL = 2
GATE_TILES = 4
ONES_ROWS = 16

LANES = 128
PROJ_BLOCK = 512
N_PROJ_BLOCKS = D_IN // PROJ_BLOCK
PB_Q, PB_K, PB_V, PB_G = 0, 2, 4, 6
PB_U_GM, PB_V_GM, PB_G_GM, PB_Q_ME, PB_G_ME = 8, 9, 10, 11, 12

INPROJ_ROWS = 1024
TAIL_ROWS = 256
VMEM_LIMIT = 48 * 1024 * 1024
MOBA_VMEM_LIMIT = 56 * 1024 * 1024

F32 = jnp.float32
BF16 = jnp.bfloat16

_NT = (((1,), (1,)), ((), ()))


def _dot(a, b):
    return jnp.dot(a, b, preferred_element_type=F32)


def _dot_nt(a, b):
    return lax.dot_general(a, b, _NT, preferred_element_type=F32)


def _lane_tile(a, reps):
    return a if reps == 1 else jnp.concatenate([a] * reps, axis=1)


def _silu(y):
    return y * (1.0 / (1.0 + jnp.exp(-y)))


def _gelu_tanh(y):
    c = np.float32(np.sqrt(2.0 / np.pi))
    return 0.5 * y * (1.0 + jnp.tanh(c * (y + 0.044715 * (y * y * y))))


def _memkv_kernel(mem_ref, wk_ref, wvt_ref, k_ref, vt_ref):
    m = mem_ref[0].astype(BF16)
    k_ref[0] = _dot(m, wk_ref[...]).astype(BF16)
    vt_ref[0] = _dot_nt(wvt_ref[...], m).astype(BF16)


def _memkv(mem, wk, wvt):
    b = mem.shape[0]
    return pl.pallas_call(
        _memkv_kernel,
        grid=(b,),
        in_specs=[
            pl.BlockSpec((1, MEM_LEN, D_MODEL), lambda i: (i, 0, 0)),
            pl.BlockSpec((D_MODEL, MEM_W), lambda i: (0, 0)),
            pl.BlockSpec((MEM_W, D_MODEL), lambda i: (0, 0)),
        ],
        out_specs=[
            pl.BlockSpec((1, MEM_LEN, MEM_W), lambda i: (i, 0, 0)),
            pl.BlockSpec((1, MEM_W, MEM_LEN), lambda i: (i, 0, 0)),
        ],
        out_shape=[
            jax.ShapeDtypeStruct((b, MEM_LEN, MEM_W), BF16),
            jax.ShapeDtypeStruct((b, MEM_W, MEM_LEN), BF16),
        ],
        compiler_params=pltpu.CompilerParams(
            dimension_semantics=("arbitrary",), vmem_limit_bytes=VMEM_LIMIT),
        name="memkv",
    )(mem, wk, wvt)


def _inproj_kernel(pos_ref, x_ref, w_ref, inv_ref, lng_ref, lnb_ref, o_ref,
                   xb_scr, cos_scr, sin_scr):
    j = pl.program_id(1)
    tm = x_ref.shape[0]
    reps = tm // LANES

    @pl.when(j == 0)
    def _():
        xb_scr[...] = x_ref[...].astype(BF16)
        pos = pos_ref[0].astype(F32)
        ang = _lane_tile(inv_ref[...], reps) * pos
        cos_scr[...] = jnp.cos(ang)
        sin_scr[...] = jnp.sin(ang)

    y = _dot_nt(w_ref[...], xb_scr[...])

    def rope_store(scale):
        half = HEAD_DIM // 2
        c = cos_scr[...]
        s = sin_scr[...]
        for h in range(PROJ_BLOCK // HEAD_DIM):
            x1 = y[h * HEAD_DIM:h * HEAD_DIM + half]
            x2 = y[h * HEAD_DIM + half:(h + 1) * HEAD_DIM]
            o1 = x1 * c - x2 * s
            o2 = x2 * c + x1 * s
            if scale != 1.0:
                o1 = o1 * scale
                o2 = o2 * scale
            o_ref[0, h * HEAD_DIM:h * HEAD_DIM + half, :] = o1.astype(BF16)
            o_ref[0, h * HEAD_DIM + half:(h + 1) * HEAD_DIM, :] = o2.astype(BF16)

    @pl.when(j < PB_K)
    def _():
        rope_store(Q_SCALE)

    @pl.when(jnp.logical_and(j >= PB_K, j < PB_V))
    def _():
        rope_store(1.0)

    @pl.when(jnp.logical_and(j >= PB_V, j < PB_G))
    def _():
        o_ref[0] = y.astype(BF16)

    is_gate = jnp.logical_or(jnp.logical_and(j >= PB_G, j < PB_U_GM),
                             jnp.logical_or(j == PB_G_GM, j == PB_G_ME))

    @pl.when(is_gate)
    def _():
        o_ref[0] = _silu(y).astype(BF16)

    @pl.when(j == PB_U_GM)
    def _():
        o_ref[0] = _gelu_tanh(y).astype(BF16)

    @pl.when(j == PB_V_GM)
    def _():
        v = _gelu_tanh(y)
        mu = jnp.mean(v, axis=0, keepdims=True)
        d = v - mu
        var = jnp.mean(d * d, axis=0, keepdims=True)
        n = d * lax.rsqrt(var + LN_EPS)
        o_ref[0] = (n * _lane_tile(lng_ref[...], reps) + _lane_tile(lnb_ref[...], reps)).astype(BF16)

    @pl.when(j == PB_Q_ME)
    def _():
        o_ref[0] = (y * Q_SCALE).astype(BF16)


def _inproj(pos3, x2, w_t, inv_freq, lng, lnb, batch, seq):
    tm = INPROJ_ROWS
    nt = seq // tm
    return pl.pallas_call(
        _inproj_kernel,
        grid=(batch * nt, N_PROJ_BLOCKS),
        in_specs=[
            pl.BlockSpec((1, 1, tm), lambda i, j: (i, 0, 0)),
            pl.BlockSpec((tm, D_MODEL), lambda i, j: (i, 0)),
            pl.BlockSpec((PROJ_BLOCK, D_MODEL), lambda i, j: (j, 0)),
            pl.BlockSpec((HEAD_DIM // 2, LANES), lambda i, j: (0, 0)),
            pl.BlockSpec((GMLP_W, LANES), lambda i, j: (0, 0)),
            pl.BlockSpec((GMLP_W, LANES), lambda i, j: (0, 0)),
        ],
        out_specs=pl.BlockSpec((1, PROJ_BLOCK, tm), lambda i, j: (i // nt, j, i % nt)),
        out_shape=jax.ShapeDtypeStruct((batch, D_IN, seq), BF16),
        scratch_shapes=[
            pltpu.VMEM((tm, D_MODEL), BF16),
            pltpu.VMEM((HEAD_DIM // 2, tm), F32),
            pltpu.VMEM((HEAD_DIM // 2, tm), F32),
        ],
        compiler_params=pltpu.CompilerParams(
            dimension_semantics=("arbitrary", "arbitrary"), vmem_limit_bytes=VMEM_LIMIT),
        name="inproj",
    )(pos3, x2, w_t, inv_freq, lng, lnb)


def _moba_kernel(q_ref, kt_ref, vt_ref, g_ref, o_ref, k_scr, km_scr, bias_scr, va_scr,
                 sa_scr, sb_scr, pa_scr, pb_scr, *, n_steps):
    seq = kt_ref.shape[2]
    blk = MOBA_BLOCK
    nb = seq // blk
    gk = MOBA_GROUP * blk

    col = lax.broadcasted_iota(jnp.int32, (blk, LANES), 1)

    def k_body(c, carry):
        start = pl.multiple_of(c * blk, blk)
        k = kt_ref[0, :, pl.ds(start, blk)].astype(F32).T
        k_scr[pl.ds(start, blk), 0:HEAD_DIM] = k.astype(BF16)
        k_scr[pl.ds(start, blk), HEAD_DIM:2 * HEAD_DIM] = jnp.where(col == c, 1.0, 0.0).astype(BF16)
        km_scr[pl.ds(c, 1), :] = jnp.sum(k, axis=0, keepdims=True) * (1.0 / blk)
        return carry

    lax.fori_loop(0, nb, k_body, 0)
    va_scr[0:HEAD_DIM, :] = vt_ref[0]
    va_scr[HEAD_DIM:HEAD_DIM + ONES_ROWS, :] = jnp.ones((ONES_ROWS, seq), BF16)

    km = km_scr[...]
    km_hi = km.astype(BF16)
    km_lo = (km - km_hi.astype(F32)).astype(BF16)
    gw = GATE_TILES * blk
    bid = lax.broadcasted_iota(jnp.int32, (nb, gw), 0)
    tile_in_chunk = lax.broadcasted_iota(jnp.int32, (nb, gw), 1) // blk

    def gate_body(c, carry):
        cols = pl.ds(pl.multiple_of(c * gw, gw), gw)
        qt = q_ref[0, :, cols]
        gate = _dot(km_hi, qt) + _dot(km_lo, qt)
        own_block = c * GATE_TILES + tile_in_chunk
        past = bid < own_block
        g = jnp.where(past, gate, NEG_INF)
        sel = jnp.zeros(gate.shape, F32)
        for _ in range(MOBA_TOPK):
            mx = jnp.max(g, axis=0, keepdims=True)
            first = jnp.min(jnp.where(g == mx, bid, nb), axis=0, keepdims=True)
            hit = bid == first
            sel = jnp.where(jnp.logical_and(hit, past), 1.0, sel)
            g = jnp.where(hit, -jnp.inf, g)
        bias = jnp.where(jnp.logical_or(sel > 0.0, bid == own_block), 0.0, NEG_INF)
        bias_scr[:, cols] = bias.astype(BF16)
        return carry

    lax.fori_loop(0, nb // GATE_TILES, gate_body, 0)

    def next_step(tile, pos):
        last = pos >= tile // MOBA_GROUP
        return (jnp.where(last, jnp.minimum(tile + 1, nb - 1), tile),
                jnp.where(last, 0, pos + 1))

    def group_start(tile, pos):
        g = jnp.where(pos == 0, tile // MOBA_GROUP, pos - 1)
        return pl.multiple_of(g * gk, gk)

    def tile_cols(tile):
        return pl.ds(pl.multiple_of(tile * blk, blk), blk)

    def score(step, s_ref):
        tile, pos = step
        cols = tile_cols(tile)
        parts = [q_ref[0, :, cols], bias_scr[:, cols]]
        if nb < HEAD_DIM:
            parts.append(jnp.zeros((HEAD_DIM - nb, blk), BF16))
        qa = jnp.concatenate(parts, axis=0)
        start = group_start(tile, pos)
        half = gk // 2
        s_ref[0:half, :] = _dot(k_scr[pl.ds(start, half), :], qa)
        s_ref[half:gk, :] = _dot(k_scr[pl.ds(start + half, half), :], qa)

    kk = lax.broadcasted_iota(jnp.int32, (blk, blk), 0)
    qq = lax.broadcasted_iota(jnp.int32, (blk, blk), 1)

    def absorb(step, s_ref, m, acc):
        tile, pos = step
        own = pl.ds(pl.multiple_of((tile % MOBA_GROUP) * blk, blk), blk)
        keep = kk <= qq + jnp.where(pos == 0, 0, blk)
        s_ref[own, :] = jnp.where(keep, s_ref[own, :], NEG_INF)
        s = s_ref[...]
        m_prev = jnp.minimum(m, jnp.where(pos == 0, NEG_INF, -NEG_INF).astype(F32))
        m_new = jnp.maximum(m_prev, jnp.max(s, axis=0, keepdims=True))
        p = jnp.exp2((s - m_new).astype(BF16))
        start = group_start(tile, pos)
        acc = jnp.exp2(m_prev - m_new) * acc + _dot(va_scr[:, pl.ds(start, gk)], p)
        cols = tile_cols(tile)
        l = acc[HEAD_DIM:HEAD_DIM + 1, :]
        o_ref[0, :, cols] = (acc[0:HEAD_DIM, :] / l * g_ref[0, :, cols].astype(F32)).astype(BF16)
        return m_new, acc

    zero = jnp.int32(0)
    t0 = (zero, zero)
    t1 = next_step(*t0)
    score(t0, sa_scr)

    def rounds(u, carry):
        a0, a1, b0, b1, m, acc = carry
        cur, nxt = (a0, a1), (b0, b1)
        for r in range(MOBA_UNROLL // 2):
            after = next_step(*nxt)
            score(nxt, sb_scr)
            m, acc = absorb(cur, sa_scr, m, acc)
            score(after, sa_scr)
            m, acc = absorb(nxt, sb_scr, m, acc)
            cur, nxt = after, next_step(*after)
        return cur + nxt + (m, acc)

    init = t0 + t1 + (jnp.full((1, blk), NEG_INF, F32),
                      jnp.zeros((HEAD_DIM + ONES_ROWS, blk), F32))
    lax.fori_loop(0, n_steps // MOBA_UNROLL, rounds, init)


def _moba(proj_t, batch, seq):
    nb = seq // MOBA_BLOCK
    assert nb <= HEAD_DIM and nb % 16 == 0, "mask rows sit beside the head dim in one 256-deep contraction"
    n_steps = sum(i // MOBA_GROUP + 1 for i in range(nb))
    assert n_steps % MOBA_UNROLL == 0 and nb % MOBA_GROUP == 0
    hb = PROJ_BLOCK // HEAD_DIM

    def head_spec(pb):
        return pl.BlockSpec((1, HEAD_DIM, seq), lambda b, h: (b, pb * hb + h, 0),
                            pipeline_mode=pl.Buffered(1))

    gk = MOBA_GROUP * MOBA_BLOCK
    return pl.pallas_call(
        functools.partial(_moba_kernel, n_steps=n_steps),
        grid=(batch, MOBA_HEADS),
        in_specs=[head_spec(PB_Q), head_spec(PB_K), head_spec(PB_V), head_spec(PB_G)],
        out_specs=pl.BlockSpec((1, HEAD_DIM, seq), lambda b, h: (b, h, 0)),
        out_shape=jax.ShapeDtypeStruct((batch, MOBA_W, seq), BF16),
        scratch_shapes=[
            pltpu.VMEM((seq, 2 * HEAD_DIM), BF16),
            pltpu.VMEM((nb, HEAD_DIM), F32),
            pltpu.VMEM((nb, seq), BF16),
            pltpu.VMEM((HEAD_DIM + ONES_ROWS, seq), BF16),
            pltpu.VMEM((gk, MOBA_BLOCK), F32),
            pltpu.VMEM((gk, MOBA_BLOCK), F32),
            pltpu.VMEM((gk, MOBA_BLOCK), BF16),
            pltpu.VMEM((gk, MOBA_BLOCK), BF16),
        ],
        compiler_params=pltpu.CompilerParams(
            dimension_semantics=("arbitrary", "arbitrary"),
            vmem_limit_bytes=MOBA_VMEM_LIMIT),
        name="moba",
    )(proj_t, proj_t, proj_t, proj_t)


def _tail_kernel(u_ref, v_ref, gg_ref, qm_ref, gm_ref, ymo_ref, x_ref, mk_ref, mvt_ref,
                 wst_ref, bs_ref, wot_ref, lng_ref, lnb_ref, o_ref, y_scr, *, alpha):
    tm = x_ref.shape[0]
    nchunk = tm // GMLP_CHUNK
    hd = HEAD_DIM

    y_scr[0:MOBA_W, :] = ymo_ref[0]

    rr = lax.broadcasted_iota(jnp.int32, (GMLP_CHUNK, GMLP_CHUNK), 0)
    cc = lax.broadcasted_iota(jnp.int32, (GMLP_CHUNK, GMLP_CHUNK), 1)
    for g in range(GMLP_GROUPS):
        rows = slice(g * hd, (g + 1) * hd)
        w_t = jnp.where(rr <= cc, wst_ref[g], 0.0).astype(BF16)
        vg = v_ref[0, rows, :]
        stacked = jnp.concatenate(
            [vg[:, n * GMLP_CHUNK:(n + 1) * GMLP_CHUNK] for n in range(nchunk)], axis=0)
        mixed = _dot(stacked, w_t)
        mixed_t = jnp.concatenate(
            [mixed[n * hd:(n + 1) * hd, :] for n in range(nchunk)], axis=1)
        mixed_t = mixed_t + _lane_tile(bs_ref[g:g + 1, :], nchunk)
        yg = u_ref[0, rows, :].astype(F32) * mixed_t * gg_ref[0, rows, :].astype(F32)
        y_scr[MOBA_W + g * hd:MOBA_W + (g + 1) * hd, :] = yg.astype(BF16)

    for h in range(MEM_HEADS):
        rows = slice(h * hd, (h + 1) * hd)
        s = _dot(mk_ref[0, :, rows], qm_ref[0, rows, :])
        m = jnp.max(s, axis=0, keepdims=True)
        p = jnp.exp2(s - m)
        l = jnp.sum(p, axis=0, keepdims=True)
        o = _dot(mvt_ref[0, rows, :], p.astype(BF16))
        ym = o / l * gm_ref[0, rows, :].astype(F32)
        y_scr[MOBA_W + GMLP_W + h * hd:MOBA_W + GMLP_W + (h + 1) * hd, :] = ym.astype(BF16)

    sub = _dot(wot_ref[...], y_scr[...]).T
    z = alpha * x_ref[...] + sub
    mu = jnp.mean(z, axis=-1, keepdims=True)
    d = z - mu
    var = jnp.mean(d * d, axis=-1, keepdims=True)
    o_ref[...] = d * lax.rsqrt(var + LN_EPS) * lng_ref[...] + lnb_ref[...]


def _tail(proj_t, ymo_t, x2, mem_k, mem_vt, ws_t, b_s, wo_t, ln_g, ln_b, batch, seq, alpha):
    tm = TAIL_ROWS
    nt = seq // tm

    def proj_spec(pb):
        return pl.BlockSpec((1, PROJ_BLOCK, tm), lambda i: (i // nt, pb, i % nt))

    return pl.pallas_call(
        functools.partial(_tail_kernel, alpha=alpha),
        grid=(batch * nt,),
        in_specs=[
            proj_spec(PB_U_GM), proj_spec(PB_V_GM), proj_spec(PB_G_GM),
            proj_spec(PB_Q_ME), proj_spec(PB_G_ME),
            pl.BlockSpec((1, MOBA_W, tm), lambda i: (i // nt, 0, i % nt)),
            pl.BlockSpec((tm, D_MODEL), lambda i: (i, 0)),
            pl.BlockSpec((1, MEM_LEN, MEM_W), lambda i: (i // nt, 0, 0)),
            pl.BlockSpec((1, MEM_W, MEM_LEN), lambda i: (i // nt, 0, 0)),
            pl.BlockSpec((GMLP_GROUPS, GMLP_CHUNK, GMLP_CHUNK), lambda i: (0, 0, 0)),
            pl.BlockSpec((GMLP_GROUPS, GMLP_CHUNK), lambda i: (0, 0)),
            pl.BlockSpec((D_MODEL, D_MIX), lambda i: (0, 0)),
            pl.BlockSpec((1, D_MODEL), lambda i: (0, 0)),
            pl.BlockSpec((1, D_MODEL), lambda i: (0, 0)),
        ],
        out_specs=pl.BlockSpec((tm, D_MODEL), lambda i: (i, 0)),
        out_shape=jax.ShapeDtypeStruct((batch * seq, D_MODEL), F32),
        scratch_shapes=[pltpu.VMEM((D_MIX, tm), BF16)],
        compiler_params=pltpu.CompilerParams(
            dimension_semantics=("arbitrary",), vmem_limit_bytes=VMEM_LIMIT),
        name="tail",
    )(proj_t, proj_t, proj_t, proj_t, proj_t, ymo_t, x2, mem_k, mem_vt, ws_t, b_s, wo_t, ln_g, ln_b)


def kernel(x, mem, positions, w_in, w_mem_kv, gmlp_ln_g, gmlp_ln_b, gmlp_w_s, gmlp_b_s,
           w_out, ln_g, ln_b):
    batch, seq, d_model = x.shape
    depth = w_in.shape[0]
    assert d_model == D_MODEL and w_in.shape[2] == D_IN
    assert seq % INPROJ_ROWS == 0 and seq % MOBA_BLOCK == 0 and seq % TAIL_ROWS == 0
    alpha = float((2.0 * depth) ** 0.25)

    half = HEAD_DIM // 2
    inv_freq = ROPE_THETA ** (-jnp.arange(half, dtype=F32) / half)
    inv_freq = jnp.broadcast_to(inv_freq[:, None], (half, LANES))
    pos3 = positions.reshape(batch * seq // INPROJ_ROWS, 1, INPROJ_ROWS)

    x2 = x.reshape(batch * seq, d_model)
    for l in range(depth):
        w_t = w_in[l].T.astype(BF16)
        wk = w_mem_kv[l][:, :MEM_W].astype(BF16)
        wvt = w_mem_kv[l][:, MEM_W:].T.astype(BF16)
        wo_t = w_out[l].T.astype(BF16)
        ws_t = jnp.swapaxes(gmlp_w_s[l], 1, 2)
        lng = jnp.broadcast_to(gmlp_ln_g[l][:, None], (GMLP_W, LANES))
        lnb = jnp.broadcast_to(gmlp_ln_b[l][:, None], (GMLP_W, LANES))

        mem_k, mem_vt = _memkv(mem, wk, wvt)
        proj_t = _inproj(pos3, x2, w_t, inv_freq, lng, lnb, batch, seq)
        ymo_t = _moba(proj_t, batch, seq)
        x2 = _tail(proj_t, ymo_t, x2, mem_k, mem_vt, ws_t, gmlp_b_s[l], wo_t,
                   ln_g[l][None, :], ln_b[l][None, :], batch, seq, alpha)
    return x2.reshape(batch, seq, d_model)
```

```python
import functools

import jax
import jax.numpy as jnp
import numpy as np
from jax import lax
from jax.experimental import pallas as pl
from jax.experimental.pallas import tpu as pltpu

D_MODEL = 2048
HEAD_DIM = 128
MOBA_HEADS = 8
MOBA_W = MOBA_HEADS * HEAD_DIM
MOBA_BLOCK = 256
MOBA_TOPK = 3
GMLP_GROUPS = 4
GMLP_W = GMLP_GROUPS * HEAD_DIM
GMLP_CHUNK = 128
MEM_HEADS = 4
MEM_W = MEM_HEADS * HEAD_DIM
MEM_LEN = 256
D_MIX = MOBA_W + GMLP_W + MEM_W
D_IN = 4 * MOBA_W + 3 * GMLP_W + 2 * MEM_W
ROPE_THETA = 10000.0
LN_EPS = 1e-5
NEG_INF = -1e30
ATTN_SCALE = HEAD_DIM ** -0.5
LOG2E = float(np.log2(np.e))
Q_SCALE = ATTN_SCALE * LOG2E
MOBA_GROUP = 8
MOBA_UNROLL = 2
GATE_TILES = 4
ONES_ROWS = 16

LANES = 128
PROJ_BLOCK = 512
N_PROJ_BLOCKS = D_IN // PROJ_BLOCK
PB_Q, PB_K, PB_V, PB_G = 0, 2, 4, 6
PB_U_GM, PB_V_GM, PB_G_GM, PB_Q_ME, PB_G_ME = 8, 9, 10, 11, 12

INPROJ_ROWS = 1024
INPROJ_CHUNK = 256
TAIL_ROWS = 512
TAIL_CHUNK = 256
VMEM_LIMIT = 48 * 1024 * 1024
MOBA_VMEM_LIMIT = 56 * 1024 * 1024

F32 = jnp.float32
BF16 = jnp.bfloat16

_NT = (((1,), (1,)), ((), ()))


def _dot(a, b):
    return jnp.dot(a, b, preferred_element_type=F32)


def _dot_nt(a, b):
    return lax.dot_general(a, b, _NT, preferred_element_type=F32)


def _lane_tile(a, reps):
    return a if reps == 1 else jnp.concatenate([a] * reps, axis=1)


def _silu(y):
    return y * (1.0 / (1.0 + jnp.exp(-y)))


def _gelu_tanh(y):
    c = np.float32(np.sqrt(2.0 / np.pi))
    return 0.5 * y * (1.0 + jnp.tanh(c * (y + 0.044715 * (y * y * y))))


def _memkv_kernel(mem_ref, wk_ref, wvt_ref, k_ref, vt_ref):
    m = mem_ref[0].astype(BF16)
    k_ref[0] = _dot(m, wk_ref[...]).astype(BF16)
    vt_ref[0] = _dot_nt(wvt_ref[...], m).astype(BF16)


def _memkv(mem, wk, wvt):
    b = mem.shape[0]
    return pl.pallas_call(
        _memkv_kernel,
        grid=(b,),
        in_specs=[
            pl.BlockSpec((1, MEM_LEN, D_MODEL), lambda i: (i, 0, 0)),
            pl.BlockSpec((D_MODEL, MEM_W), lambda i: (0, 0)),
            pl.BlockSpec((MEM_W, D_MODEL), lambda i: (0, 0)),
        ],
        out_specs=[
            pl.BlockSpec((1, MEM_LEN, MEM_W), lambda i: (i, 0, 0)),
            pl.BlockSpec((1, MEM_W, MEM_LEN), lambda i: (i, 0, 0)),
        ],
        out_shape=[
            jax.ShapeDtypeStruct((b, MEM_LEN, MEM_W), BF16),
            jax.ShapeDtypeStruct((b, MEM_W, MEM_LEN), BF16),
        ],
        compiler_params=pltpu.CompilerParams(
            dimension_semantics=("arbitrary",), vmem_limit_bytes=VMEM_LIMIT),
        name="memkv",
    )(mem, wk, wvt)


def _inproj_kernel(pos_ref, x_ref, w_ref, inv_ref, lng_ref, lnb_ref, o_ref,
                   xb_scr, cos_scr, sin_scr):
    j = pl.program_id(1)
    tm = x_ref.shape[0]
    reps = tm // LANES

    @pl.when(j == 0)
    def _():
        xb_scr[...] = x_ref[...].astype(BF16)
        pos = pos_ref[0].astype(F32)
        ang = _lane_tile(inv_ref[...], reps) * pos
        cos_scr[...] = jnp.cos(ang)
        sin_scr[...] = jnp.sin(ang)

    ch = INPROJ_CHUNK
    creps = ch // LANES

    def chunked(epilogue):
        for c in range(tm // ch):
            cs = slice(c * ch, (c + 1) * ch)
            y = _dot_nt(w_ref[...], xb_scr[cs, :])
            epilogue(y, cs)

    def rope_store(scale):
        def epilogue(y, cs):
            half = HEAD_DIM // 2
            c = cos_scr[:, cs]
            s = sin_scr[:, cs]
            for h in range(PROJ_BLOCK // HEAD_DIM):
                x1 = y[h * HEAD_DIM:h * HEAD_DIM + half]
                x2 = y[h * HEAD_DIM + half:(h + 1) * HEAD_DIM]
                o1 = x1 * c - x2 * s
                o2 = x2 * c + x1 * s
                if scale != 1.0:
                    o1 = o1 * scale
                    o2 = o2 * scale
                o_ref[0, h * HEAD_DIM:h * HEAD_DIM + half, cs] = o1.astype(BF16)
                o_ref[0, h * HEAD_DIM + half:(h + 1) * HEAD_DIM, cs] = o2.astype(BF16)
        return epilogue

    def store(fn):
        def epilogue(y, cs):
            o_ref[0, :, cs] = fn(y).astype(BF16)
        return epilogue

    def gelu_ln(y):
        v = _gelu_tanh(y)
        mu = jnp.mean(v, axis=0, keepdims=True)
        d = v - mu
        var = jnp.mean(d * d, axis=0, keepdims=True)
        n = d * lax.rsqrt(var + LN_EPS)
        return n * _lane_tile(lng_ref[...], creps) + _lane_tile(lnb_ref[...], creps)

    @pl.when(j < PB_K)
    def _():
        chunked(rope_store(Q_SCALE))

    @pl.when(jnp.logical_and(j >= PB_K, j < PB_V))
    def _():
        chunked(rope_store(1.0))

    @pl.when(jnp.logical_and(j >= PB_V, j < PB_G))
    def _():
        chunked(store(lambda y: y))

    is_gate = jnp.logical_or(jnp.logical_and(j >= PB_G, j < PB_U_GM),
                             jnp.logical_or(j == PB_G_GM, j == PB_G_ME))

    @pl.when(is_gate)
    def _():
        chunked(store(_silu))

    @pl.when(j == PB_U_GM)
    def _():
        chunked(store(_gelu_tanh))

    @pl.when(j == PB_V_GM)
    def _():
        chunked(store(gelu_ln))

    @pl.when(j == PB_Q_ME)
    def _():
        chunked(store(lambda y: y * Q_SCALE))


def _inproj(pos3, x2, w_t, inv_freq, lng, lnb, batch, seq):
    tm = INPROJ_ROWS
    nt = seq // tm
    return pl.pallas_call(
        _inproj_kernel,
        grid=(batch * nt, N_PROJ_BLOCKS),
        in_specs=[
            pl.BlockSpec((1, 1, tm), lambda i, j: (i, 0, 0)),
            pl.BlockSpec((tm, D_MODEL), lambda i, j: (i, 0)),
            pl.BlockSpec((PROJ_BLOCK, D_MODEL), lambda i, j: (j, 0)),
            pl.BlockSpec((HEAD_DIM // 2, LANES), lambda i, j: (0, 0)),
            pl.BlockSpec((GMLP_W, LANES), lambda i, j: (0, 0)),
            pl.BlockSpec((GMLP_W, LANES), lambda i, j: (0, 0)),
        ],
        out_specs=pl.BlockSpec((1, PROJ_BLOCK, tm), lambda i, j: (i // nt, j, i % nt)),
        out_shape=jax.ShapeDtypeStruct((batch, D_IN, seq), BF16),
        scratch_shapes=[
            pltpu.VMEM((tm, D_MODEL), BF16),
            pltpu.VMEM((HEAD_DIM // 2, tm), F32),
            pltpu.VMEM((HEAD_DIM // 2, tm), F32),
        ],
        compiler_params=pltpu.CompilerParams(
            dimension_semantics=("arbitrary", "arbitrary"), vmem_limit_bytes=VMEM_LIMIT),
        name="inproj",
    )(pos3, x2, w_t, inv_freq, lng, lnb)


def _moba_kernel(q_ref, kt_ref, vt_ref, g_ref, o_ref, k_scr, km_scr, bias_scr, va_scr,
                 sa_scr, sb_scr, *, n_steps):
    seq = kt_ref.shape[2]
    blk = MOBA_BLOCK
    nb = seq // blk
    gk = MOBA_GROUP * blk

    col = lax.broadcasted_iota(jnp.int32, (blk, LANES), 1)

    def k_body(c, carry):
        start = pl.multiple_of(c * blk, blk)
        k = kt_ref[0, :, pl.ds(start, blk)].astype(F32).T
        k_scr[pl.ds(start, blk), 0:HEAD_DIM] = k.astype(BF16)
        k_scr[pl.ds(start, blk), HEAD_DIM:2 * HEAD_DIM] = jnp.where(col == c, 1.0, 0.0).astype(BF16)
        km_scr[pl.ds(c, 1), :] = jnp.sum(k, axis=0, keepdims=True) * (1.0 / blk)
        return carry

    lax.fori_loop(0, nb, k_body, 0)
    va_scr[0:HEAD_DIM, :] = vt_ref[0]
    va_scr[HEAD_DIM:HEAD_DIM + ONES_ROWS, :] = jnp.ones((ONES_ROWS, seq), BF16)

    km = km_scr[...]
    km_hi = km.astype(BF16)
    km_lo = (km - km_hi.astype(F32)).astype(BF16)
    gw = GATE_TILES * blk
    bid = lax.broadcasted_iota(jnp.int32, (nb, gw), 0)
    tile_in_chunk = lax.broadcasted_iota(jnp.int32, (nb, gw), 1) // blk

    def gate_body(c, carry):
        cols = pl.ds(pl.multiple_of(c * gw, gw), gw)
        qt = q_ref[0, :, cols]
        gate = _dot(km_hi, qt) + _dot(km_lo, qt)
        own_block = c * GATE_TILES + tile_in_chunk
        past = bid < own_block
        g = jnp.where(past, gate, NEG_INF)
        sel = jnp.zeros(gate.shape, F32)
        for _ in range(MOBA_TOPK):
            mx = jnp.max(g, axis=0, keepdims=True)
            first = jnp.min(jnp.where(g == mx, bid, nb), axis=0, keepdims=True)
            hit = bid == first
            sel = jnp.where(jnp.logical_and(hit, past), 1.0, sel)
            g = jnp.where(hit, -jnp.inf, g)
        bias = jnp.where(jnp.logical_or(sel > 0.0, bid == own_block), 0.0, NEG_INF)
        bias_scr[:, cols] = bias.astype(BF16)
        return carry

    lax.fori_loop(0, nb // GATE_TILES, gate_body, 0)

    def next_step(tile, pos):
        last = pos >= tile // MOBA_GROUP
        return (jnp.where(last, jnp.minimum(tile + 1, nb - 1), tile),
                jnp.where(last, 0, pos + 1))

    def group_start(tile, pos):
        g = jnp.where(pos == 0, tile // MOBA_GROUP, pos - 1)
        return pl.multiple_of(g * gk, gk)

    def tile_cols(tile):
        return pl.ds(pl.multiple_of(tile * blk, blk), blk)

    def score(step, s_ref):
        tile, pos = step
        cols = tile_cols(tile)
        parts = [q_ref[0, :, cols], bias_scr[:, cols]]
        if nb < HEAD_DIM:
            parts.append(jnp.zeros((HEAD_DIM - nb, blk), BF16))
        qa = jnp.concatenate(parts, axis=0)
        start = group_start(tile, pos)
        half = gk // 2
        s_ref[0:half, :] = _dot(k_scr[pl.ds(start, half), :], qa)
        s_ref[half:gk, :] = _dot(k_scr[pl.ds(start + half, half), :], qa)

    kk = lax.broadcasted_iota(jnp.int32, (blk, blk), 0)
    qq = lax.broadcasted_iota(jnp.int32, (blk, blk), 1)

    def absorb(step, s_ref, m, acc):
        tile, pos = step
        own = pl.ds(pl.multiple_of((tile % MOBA_GROUP) * blk, blk), blk)
        keep = kk <= qq + jnp.where(pos == 0, 0, blk)
        s_ref[own, :] = jnp.where(keep, s_ref[own, :], NEG_INF)
        s = s_ref[...]
        m_prev = jnp.minimum(m, jnp.where(pos == 0, NEG_INF, -NEG_INF).astype(F32))
        m_new = jnp.maximum(m_prev, jnp.max(s, axis=0, keepdims=True))
        p = jnp.exp2((s - m_new).astype(BF16))
        start = group_start(tile, pos)
        acc = jnp.exp2(m_prev - m_new) * acc + _dot(va_scr[:, pl.ds(start, gk)], p)
        cols = tile_cols(tile)
        l = acc[HEAD_DIM:HEAD_DIM + 1, :]
        o_ref[0, :, cols] = (acc[0:HEAD_DIM, :] / l * g_ref[0, :, cols].astype(F32)).astype(BF16)
        return m_new, acc

    zero = jnp.int32(0)
    t0 = (zero, zero)
    t1 = next_step(*t0)
    score(t0, sa_scr)

    def rounds(u, carry):
        a0, a1, b0, b1, m, acc = carry
        cur, nxt = (a0, a1), (b0, b1)
        for r in range(MOBA_UNROLL // 2):
            after = next_step(*nxt)
            score(nxt, sb_scr)
            m, acc = absorb(cur, sa_scr, m, acc)
            score(after, sa_scr)
            m, acc = absorb(nxt, sb_scr, m, acc)
            cur, nxt = after, next_step(*after)
        return cur + nxt + (m, acc)

    init = t0 + t1 + (jnp.full((1, blk), NEG_INF, F32),
                      jnp.zeros((HEAD_DIM + ONES_ROWS, blk), F32))
    lax.fori_loop(0, n_steps // MOBA_UNROLL, rounds, init)


def _moba(proj_t, batch, seq):
    nb = seq // MOBA_BLOCK
    assert nb <= HEAD_DIM and nb % 16 == 0, "mask rows sit beside the head dim in one 256-deep contraction"
    n_steps = sum(i // MOBA_GROUP + 1 for i in range(nb))
    assert n_steps % MOBA_UNROLL == 0 and nb % MOBA_GROUP == 0
    hb = PROJ_BLOCK // HEAD_DIM

    def head_spec(pb):
        return pl.BlockSpec((1, HEAD_DIM, seq), lambda b, h: (b, pb * hb + h, 0),
                            pipeline_mode=pl.Buffered(1))

    gk = MOBA_GROUP * MOBA_BLOCK
    return pl.pallas_call(
        functools.partial(_moba_kernel, n_steps=n_steps),
        grid=(batch, MOBA_HEADS),
        in_specs=[head_spec(PB_Q), head_spec(PB_K), head_spec(PB_V), head_spec(PB_G)],
        out_specs=pl.BlockSpec((1, HEAD_DIM, seq), lambda b, h: (b, h, 0)),
        out_shape=jax.ShapeDtypeStruct((batch, MOBA_W, seq), BF16),
        scratch_shapes=[
            pltpu.VMEM((seq, 2 * HEAD_DIM), BF16),
            pltpu.VMEM((nb, HEAD_DIM), F32),
            pltpu.VMEM((nb, seq), BF16),
            pltpu.VMEM((HEAD_DIM + ONES_ROWS, seq), BF16),
            pltpu.VMEM((gk, MOBA_BLOCK), F32),
            pltpu.VMEM((gk, MOBA_BLOCK), F32),
        ],
        compiler_params=pltpu.CompilerParams(
            dimension_semantics=("arbitrary", "arbitrary"),
            vmem_limit_bytes=MOBA_VMEM_LIMIT),
        name="moba",
    )(proj_t, proj_t, proj_t, proj_t)


def _tail_kernel(u_ref, v_ref, gg_ref, qm_ref, gm_ref, ymo_ref, x_ref, mk_ref, mvt_ref,
                 wst_ref, bs_ref, wot_ref, lng_ref, lnb_ref, o_ref, y_scr, *, alpha):
    tm = x_ref.shape[0]
    ch = TAIL_CHUNK
    nsub = ch // GMLP_CHUNK
    hd = HEAD_DIM
    rr = lax.broadcasted_iota(jnp.int32, (GMLP_CHUNK, GMLP_CHUNK), 0)
    cc = lax.broadcasted_iota(jnp.int32, (GMLP_CHUNK, GMLP_CHUNK), 1)

    for c in range(tm // ch):
        cs = slice(c * ch, (c + 1) * ch)
        y_scr[0:MOBA_W, cs] = ymo_ref[0, :, cs]

        for g in range(GMLP_GROUPS):
            rows = slice(g * hd, (g + 1) * hd)
            w_t = jnp.where(rr <= cc, wst_ref[g], 0.0).astype(BF16)
            vg = v_ref[0, rows, cs]
            stacked = jnp.concatenate(
                [vg[:, n * GMLP_CHUNK:(n + 1) * GMLP_CHUNK] for n in range(nsub)], axis=0)
            mixed = _dot(stacked, w_t)
            mixed_t = jnp.concatenate(
                [mixed[n * hd:(n + 1) * hd, :] for n in range(nsub)], axis=1)
            mixed_t = mixed_t + _lane_tile(bs_ref[g:g + 1, :], nsub)
            yg = u_ref[0, rows, cs].astype(F32) * mixed_t * gg_ref[0, rows, cs].astype(F32)
            y_scr[MOBA_W + g * hd:MOBA_W + (g + 1) * hd, cs] = yg.astype(BF16)

        for h in range(MEM_HEADS):
            rows = slice(h * hd, (h + 1) * hd)
            s = _dot(mk_ref[0, :, rows], qm_ref[0, rows, cs])
            m = jnp.max(s, axis=0, keepdims=True)
            p = jnp.exp2(s - m)
            l = jnp.sum(p, axis=0, keepdims=True)
            o = _dot(mvt_ref[0, rows, :], p.astype(BF16))
            ym = o / l * gm_ref[0, rows, cs].astype(F32)
            y_scr[MOBA_W + GMLP_W + h * hd:MOBA_W + GMLP_W + (h + 1) * hd, cs] = ym.astype(BF16)

        sub = _dot(wot_ref[...], y_scr[:, cs]).T
        z = alpha * x_ref[cs, :] + sub
        mu = jnp.mean(z, axis=-1, keepdims=True)
        d = z - mu
        var = jnp.mean(d * d, axis=-1, keepdims=True)
        o_ref[cs, :] = d * lax.rsqrt(var + LN_EPS) * lng_ref[...] + lnb_ref[...]


def _tail(proj_t, ymo_t, x2, mem_k, mem_vt, ws_t, b_s, wo_t, ln_g, ln_b, batch, seq, alpha):
    tm = TAIL_ROWS
    nt = seq // tm

    def proj_spec(pb):
        return pl.BlockSpec((1, PROJ_BLOCK, tm), lambda i: (i // nt, pb, i % nt))

    return pl.pallas_call(
        functools.partial(_tail_kernel, alpha=alpha),
        grid=(batch * nt,),
        in_specs=[
            proj_spec(PB_U_GM), proj_spec(PB_V_GM), proj_spec(PB_G_GM),
            proj_spec(PB_Q_ME), proj_spec(PB_G_ME),
            pl.BlockSpec((1, MOBA_W, tm), lambda i: (i // nt, 0, i % nt)),
            pl.BlockSpec((tm, D_MODEL), lambda i: (i, 0)),
            pl.BlockSpec((1, MEM_LEN, MEM_W), lambda i: (i // nt, 0, 0)),
            pl.BlockSpec((1, MEM_W, MEM_LEN), lambda i: (i // nt, 0, 0)),
            pl.BlockSpec((GMLP_GROUPS, GMLP_CHUNK, GMLP_CHUNK), lambda i: (0, 0, 0)),
            pl.BlockSpec((GMLP_GROUPS, GMLP_CHUNK), lambda i: (0, 0)),
            pl.BlockSpec((D_MODEL, D_MIX), lambda i: (0, 0), pipeline_mode=pl.Buffered(1)),
            pl.BlockSpec((1, D_MODEL), lambda i: (0, 0)),
            pl.BlockSpec((1, D_MODEL), lambda i: (0, 0)),
        ],
        out_specs=pl.BlockSpec((tm, D_MODEL), lambda i: (i, 0)),
        out_shape=jax.ShapeDtypeStruct((batch * seq, D_MODEL), F32),
        scratch_shapes=[pltpu.VMEM((D_MIX, tm), BF16)],
        compiler_params=pltpu.CompilerParams(
            dimension_semantics=("arbitrary",), vmem_limit_bytes=VMEM_LIMIT),
        name="tail",
    )(proj_t, proj_t, proj_t, proj_t, proj_t, ymo_t, x2, mem_k, mem_vt, ws_t, b_s, wo_t, ln_g, ln_b)


def kernel(x, mem, positions, w_in, w_mem_kv, gmlp_ln_g, gmlp_ln_b, gmlp_w_s, gmlp_b_s,
           w_out, ln_g, ln_b):
    batch, seq, d_model = x.shape
    depth = w_in.shape[0]
    assert d_model == D_MODEL and w_in.shape[2] == D_IN
    assert seq % INPROJ_ROWS == 0 and seq % MOBA_BLOCK == 0 and seq % TAIL_ROWS == 0
    alpha = float((2.0 * depth) ** 0.25)

    half = HEAD_DIM // 2
    inv_freq = ROPE_THETA ** (-jnp.arange(half, dtype=F32) / half)
    inv_freq = jnp.broadcast_to(inv_freq[:, None], (half, LANES))
    pos3 = positions.reshape(batch * seq // INPROJ_ROWS, 1, INPROJ_ROWS)

    x2 = x.reshape(batch * seq, d_model)
    for l in range(depth):
        w_t = w_in[l].T.astype(BF16)
        wk = w_mem_kv[l][:, :MEM_W].astype(BF16)
        wvt = w_mem_kv[l][:, MEM_W:].T.astype(BF16)
        wo_t = w_out[l].T.astype(BF16)
        ws_t = jnp.swapaxes(gmlp_w_s[l], 1, 2)
        lng = jnp.broadcast_to(gmlp_ln_g[l][:, None], (GMLP_W, LANES))
        lnb = jnp.broadcast_to(gmlp_ln_b[l][:, None], (GMLP_W, LANES))

        mem_k, mem_vt = _memkv(mem, wk, wvt)
        proj_t = _inproj(pos3, x2, w_t, inv_freq, lng, lnb, batch, seq)
        ymo_t = _moba(proj_t, batch, seq)
        x2 = _tail(proj_t, ymo_t, x2, mem_k, mem_vt, ws_t, gmlp_b_s[l], wo_t,
                   ln_g[l][None, :], ln_b[l][None, :], batch, seq, alpha)
    return x2.reshape(batch, seq, d_model)
```

```python
import functools

import jax
import jax.numpy as jnp
import numpy as np
from jax import lax
from jax.experimental import pallas as pl
from jax.experimental.pallas import tpu as pltpu

D_MODEL = 2048
HEAD_DIM = 128
MOBA_HEADS = 8
MOBA_W = MOBA_HEADS * HEAD_DIM
MOBA_BLOCK = 256
MOBA_TOPK = 3
GMLP_GROUPS = 4
GMLP_W = GMLP_GROUPS * HEAD_DIM
GMLP_CHUNK = 128
MEM_HEADS = 4
MEM_W = MEM_HEADS * HEAD_DIM
MEM_LEN = 256
D_MIX = MOBA_W + GMLP_W + MEM_W
D_IN = 4 * MOBA_W + 3 * GMLP_W + 2 * MEM_W
ROPE_THETA = 10000.0
LN_EPS = 1e-5
NEG_INF = -1e30
ATTN_SCALE = HEAD_DIM ** -0.5
LOG2E = float(np.log2(np.e))
Q_SCALE = ATTN_SCALE * LOG2E
MOBA_GROUP = 8
QK_PARTS = 2
MOBA_UNROLL = 2
K_UNROLL = 4
GATE_TILES = 4
ONES_ROWS = 16

LANES = 128
PROJ_BLOCK = 512
N_PROJ_BLOCKS = D_IN // PROJ_BLOCK
PB_Q, PB_K, PB_V, PB_G = 0, 2, 4, 6
PB_U_GM, PB_V_GM, PB_G_GM, PB_Q_ME, PB_G_ME = 8, 9, 10, 11, 12

INPROJ_ROWS = 1024
INPROJ_CHUNK = 256
TAIL_ROWS = 512
TAIL_CHUNK = 256
VMEM_LIMIT = 48 * 1024 * 1024
MOBA_VMEM_LIMIT = 56 * 1024 * 1024

F32 = jnp.float32
BF16 = jnp.bfloat16

_NT = (((1,), (1,)), ((), ()))


def _dot(a, b):
    return jnp.dot(a, b, preferred_element_type=F32)


def _dot_nt(a, b):
    return lax.dot_general(a, b, _NT, preferred_element_type=F32)


def _lane_tile(a, reps):
    return a if reps == 1 else jnp.concatenate([a] * reps, axis=1)


def _silu(y):
    return y * (1.0 / (1.0 + jnp.exp(-y)))


def _gelu_tanh(y):
    c = np.float32(np.sqrt(2.0 / np.pi))
    return 0.5 * y * (1.0 + jnp.tanh(c * (y + 0.044715 * (y * y * y))))


def _memkv_kernel(mem_ref, wk_ref, wvt_ref, k_ref, vt_ref):
    m = mem_ref[0].astype(BF16)
    k_ref[0] = _dot(m, wk_ref[...]).astype(BF16)
    vt_ref[0] = _dot_nt(wvt_ref[...], m).astype(BF16)


def _memkv(mem, wk, wvt):
    b = mem.shape[0]
    return pl.pallas_call(
        _memkv_kernel,
        grid=(b,),
        in_specs=[
            pl.BlockSpec((1, MEM_LEN, D_MODEL), lambda i: (i, 0, 0)),
            pl.BlockSpec((D_MODEL, MEM_W), lambda i: (0, 0)),
            pl.BlockSpec((MEM_W, D_MODEL), lambda i: (0, 0)),
        ],
        out_specs=[
            pl.BlockSpec((1, MEM_LEN, MEM_W), lambda i: (i, 0, 0)),
            pl.BlockSpec((1, MEM_W, MEM_LEN), lambda i: (i, 0, 0)),
        ],
        out_shape=[
            jax.ShapeDtypeStruct((b, MEM_LEN, MEM_W), BF16),
            jax.ShapeDtypeStruct((b, MEM_W, MEM_LEN), BF16),
        ],
        compiler_params=pltpu.CompilerParams(
            dimension_semantics=("arbitrary",), vmem_limit_bytes=VMEM_LIMIT),
        name="memkv",
    )(mem, wk, wvt)


def _inproj_kernel(pos_ref, x_ref, w_ref, inv_ref, lng_ref, lnb_ref, o_ref,
                   xb_scr, cos_scr, sin_scr):
    j = pl.program_id(1)
    tm = x_ref.shape[0]

    ch = INPROJ_CHUNK
    creps = ch // LANES

    def chunked(epilogue, first=False):
        for c in range(tm // ch):
            cs = slice(c * ch, (c + 1) * ch)
            if first:
                xb_scr[cs, :] = x_ref[cs, :].astype(BF16)
                ang = _lane_tile(inv_ref[...], creps) * pos_ref[0, :, cs].astype(F32)
                cos_scr[:, cs] = jnp.cos(ang)
                sin_scr[:, cs] = jnp.sin(ang)
            y = _dot_nt(w_ref[...], xb_scr[cs, :])
            epilogue(y, cs)

    def rope_store(scale):
        def epilogue(y, cs):
            half = HEAD_DIM // 2
            c = cos_scr[:, cs]
            s = sin_scr[:, cs]
            for h in range(PROJ_BLOCK // HEAD_DIM):
                x1 = y[h * HEAD_DIM:h * HEAD_DIM + half]
                x2 = y[h * HEAD_DIM + half:(h + 1) * HEAD_DIM]
                o1 = x1 * c - x2 * s
                o2 = x2 * c + x1 * s
                if scale != 1.0:
                    o1 = o1 * scale
                    o2 = o2 * scale
                o_ref[0, h * HEAD_DIM:h * HEAD_DIM + half, cs] = o1.astype(BF16)
                o_ref[0, h * HEAD_DIM + half:(h + 1) * HEAD_DIM, cs] = o2.astype(BF16)
        return epilogue

    def store(fn):
        def epilogue(y, cs):
            o_ref[0, :, cs] = fn(y).astype(BF16)
        return epilogue

    def gelu_ln(y):
        v = _gelu_tanh(y)
        mu = jnp.mean(v, axis=0, keepdims=True)
        d = v - mu
        var = jnp.mean(d * d, axis=0, keepdims=True)
        n = d * lax.rsqrt(var + LN_EPS)
        return n * _lane_tile(lng_ref[...], creps) + _lane_tile(lnb_ref[...], creps)

    @pl.when(j == 0)
    def _():
        chunked(rope_store(Q_SCALE), first=True)

    @pl.when(jnp.logical_and(j > 0, j < PB_K))
    def _():
        chunked(rope_store(Q_SCALE))

    @pl.when(jnp.logical_and(j >= PB_K, j < PB_V))
    def _():
        chunked(rope_store(1.0))

    @pl.when(jnp.logical_and(j >= PB_V, j < PB_G))
    def _():
        chunked(store(lambda y: y))

    is_gate = jnp.logical_or(jnp.logical_and(j >= PB_G, j < PB_U_GM),
                             jnp.logical_or(j == PB_G_GM, j == PB_G_ME))

    @pl.when(is_gate)
    def _():
        chunked(store(_silu))

    @pl.when(j == PB_U_GM)
    def _():
        chunked(store(_gelu_tanh))

    @pl.when(j == PB_V_GM)
    def _():
        chunked(store(gelu_ln))

    @pl.when(j == PB_Q_ME)
    def _():
        chunked(store(lambda y: y * Q_SCALE))


def _inproj(pos3, x2, w_t, inv_freq, lng, lnb, batch, seq):
    tm = INPROJ_ROWS
    nt = seq // tm
    return pl.pallas_call(
        _inproj_kernel,
        grid=(batch * nt, N_PROJ_BLOCKS),
        in_specs=[
            pl.BlockSpec((1, 1, tm), lambda i, j: (i, 0, 0)),
            pl.BlockSpec((tm, D_MODEL), lambda i, j: (i, 0)),
            pl.BlockSpec((PROJ_BLOCK, D_MODEL), lambda i, j: (j, 0)),
            pl.BlockSpec((HEAD_DIM // 2, LANES), lambda i, j: (0, 0)),
            pl.BlockSpec((GMLP_W, LANES), lambda i, j: (0, 0)),
            pl.BlockSpec((GMLP_W, LANES), lambda i, j: (0, 0)),
        ],
        out_specs=pl.BlockSpec((1, PROJ_BLOCK, tm), lambda i, j: (i // nt, j, i % nt)),
        out_shape=jax.ShapeDtypeStruct((batch, D_IN, seq), BF16),
        scratch_shapes=[
            pltpu.VMEM((tm, D_MODEL), BF16),
            pltpu.VMEM((HEAD_DIM // 2, tm), F32),
            pltpu.VMEM((HEAD_DIM // 2, tm), F32),
        ],
        compiler_params=pltpu.CompilerParams(
            dimension_semantics=("arbitrary", "arbitrary"), vmem_limit_bytes=VMEM_LIMIT),
        name="inproj",
    )(pos3, x2, w_t, inv_freq, lng, lnb)


def _moba_kernel(q_ref, kt_ref, vt_ref, g_ref, o_ref, k_scr, km_scr, bias_scr, va_scr,
                 sa_scr, sb_scr, *, n_steps):
    seq = kt_ref.shape[2]
    blk = MOBA_BLOCK
    nb = seq // blk
    gk = MOBA_GROUP * blk

    col = lax.broadcasted_iota(jnp.int32, (blk, LANES), 1)

    def k_body(c4, carry):
        for r in range(K_UNROLL):
            c = c4 * K_UNROLL + r
            start = pl.multiple_of(c * blk, blk)
            k = kt_ref[0, :, pl.ds(start, blk)].astype(F32).T
            k_scr[pl.ds(start, blk), 0:HEAD_DIM] = k.astype(BF16)
            k_scr[pl.ds(start, blk), HEAD_DIM:2 * HEAD_DIM] = jnp.where(col == c, 1.0, 0.0).astype(BF16)
            km_scr[pl.ds(c, 1), :] = jnp.sum(k, axis=0, keepdims=True) * (1.0 / blk)
        return carry

    lax.fori_loop(0, nb // K_UNROLL, k_body, 0)
    va_scr[0:HEAD_DIM, :] = vt_ref[0]
    va_scr[HEAD_DIM:HEAD_DIM + ONES_ROWS, :] = jnp.ones((ONES_ROWS, seq), BF16)

    km = km_scr[...]
    km_hi = km.astype(BF16)
    km_lo = (km - km_hi.astype(F32)).astype(BF16)
    gw = GATE_TILES * blk
    bid = lax.broadcasted_iota(jnp.int32, (nb, gw), 0)
    tile_in_chunk = lax.broadcasted_iota(jnp.int32, (nb, gw), 1) // blk

    def gate_body(c, carry):
        cols = pl.ds(pl.multiple_of(c * gw, gw), gw)
        qt = q_ref[0, :, cols]
        gate = _dot(km_hi, qt) + _dot(km_lo, qt)
        own_block = c * GATE_TILES + tile_in_chunk
        past = bid < own_block
        g = jnp.where(past, gate, NEG_INF)
        sel = jnp.zeros(gate.shape, F32)
        for _ in range(MOBA_TOPK):
            mx = jnp.max(g, axis=0, keepdims=True)
            first = jnp.min(jnp.where(g == mx, bid, nb), axis=0, keepdims=True)
            hit = bid == first
            sel = jnp.where(jnp.logical_and(hit, past), 1.0, sel)
            g = jnp.where(hit, -jnp.inf, g)
        bias = jnp.where(jnp.logical_or(sel > 0.0, bid == own_block), 0.0, NEG_INF)
        bias_scr[:, cols] = bias.astype(BF16)
        return carry

    lax.fori_loop(0, nb // GATE_TILES, gate_body, 0)

    def next_step(tile, pos):
        last = pos >= tile // MOBA_GROUP
        return (jnp.where(last, jnp.minimum(tile + 1, nb - 1), tile),
                jnp.where(last, 0, pos + 1))

    def group_start(tile, pos):
        g = jnp.where(pos == 0, tile // MOBA_GROUP, pos - 1)
        return pl.multiple_of(g * gk, gk)

    def tile_cols(tile):
        return pl.ds(pl.multiple_of(tile * blk, blk), blk)

    def score(step, s_ref):
        tile, pos = step
        cols = tile_cols(tile)
        parts = [q_ref[0, :, cols], bias_scr[:, cols]]
        if nb < HEAD_DIM:
            parts.append(jnp.zeros((HEAD_DIM - nb, blk), BF16))
        qa = jnp.concatenate(parts, axis=0)
        start = group_start(tile, pos)
        part = gk // QK_PARTS
        for r in range(QK_PARTS):
            s_ref[r * part:(r + 1) * part, :] = _dot(k_scr[pl.ds(start + r * part, part), :], qa)

    kk = lax.broadcasted_iota(jnp.int32, (blk, blk), 0)
    qq = lax.broadcasted_iota(jnp.int32, (blk, blk), 1)

    def absorb(step, s_ref, m, acc):
        tile, pos = step
        own = pl.ds(pl.multiple_of((tile % MOBA_GROUP) * blk, blk), blk)
        keep = kk <= qq + jnp.where(pos == 0, 0, blk)
        s_ref[own, :] = jnp.where(keep, s_ref[own, :], NEG_INF)
        s = s_ref[...]
        m_prev = jnp.minimum(m, jnp.where(pos == 0, NEG_INF, -NEG_INF).astype(F32))
        m_new = jnp.maximum(m_prev, jnp.max(s, axis=0, keepdims=True))
        p = jnp.exp2((s - m_new).astype(BF16))
        start = group_start(tile, pos)
        acc = jnp.exp2(m_prev - m_new) * acc + _dot(va_scr[:, pl.ds(start, gk)], p)
        cols = tile_cols(tile)
        l = acc[HEAD_DIM:HEAD_DIM + 1, :]
        o_ref[0, :, cols] = (acc[0:HEAD_DIM, :] / l * g_ref[0, :, cols].astype(F32)).astype(BF16)
        return m_new, acc

    zero = jnp.int32(0)
    t0 = (zero, zero)
    t1 = next_step(*t0)
    score(t0, sa_scr)

    def rounds(u, carry):
        a0, a1, b0, b1, m, acc = carry
        cur, nxt = (a0, a1), (b0, b1)
        for r in range(MOBA_UNROLL // 2):
            after = next_step(*nxt)
            score(nxt, sb_scr)
            m, acc = absorb(cur, sa_scr, m, acc)
            score(after, sa_scr)
            m, acc = absorb(nxt, sb_scr, m, acc)
            cur, nxt = after, next_step(*after)
        return cur + nxt + (m, acc)

    init = t0 + t1 + (jnp.full((1, blk), NEG_INF, F32),
                      jnp.zeros((HEAD_DIM + ONES_ROWS, blk), F32))
    lax.fori_loop(0, n_steps // MOBA_UNROLL, rounds, init)


def _moba(proj_t, batch, seq):
    nb = seq // MOBA_BLOCK
    assert nb <= HEAD_DIM and nb % 16 == 0, "mask rows sit beside the head dim in one 256-deep contraction"
    n_steps = sum(i // MOBA_GROUP + 1 for i in range(nb))
    assert n_steps % MOBA_UNROLL == 0 and nb % MOBA_GROUP == 0
    hb = PROJ_BLOCK // HEAD_DIM

    def head_spec(pb):
        return pl.BlockSpec((1, HEAD_DIM, seq), lambda b, h: (b, pb * hb + h, 0),
                            pipeline_mode=pl.Buffered(1))

    gk = MOBA_GROUP * MOBA_BLOCK
    return pl.pallas_call(
        functools.partial(_moba_kernel, n_steps=n_steps),
        grid=(batch, MOBA_HEADS),
        in_specs=[head_spec(PB_Q), head_spec(PB_K), head_spec(PB_V), head_spec(PB_G)],
        out_specs=pl.BlockSpec((1, HEAD_DIM, seq), lambda b, h: (b, h, 0)),
        out_shape=jax.ShapeDtypeStruct((batch, MOBA_W, seq), BF16),
        scratch_shapes=[
            pltpu.VMEM((seq, 2 * HEAD_DIM), BF16),
            pltpu.VMEM((nb, HEAD_DIM), F32),
            pltpu.VMEM((nb, seq), BF16),
            pltpu.VMEM((HEAD_DIM + ONES_ROWS, seq), BF16),
            pltpu.VMEM((gk, MOBA_BLOCK), F32),
            pltpu.VMEM((gk, MOBA_BLOCK), F32),
        ],
        compiler_params=pltpu.CompilerParams(
            dimension_semantics=("arbitrary", "arbitrary"),
            vmem_limit_bytes=MOBA_VMEM_LIMIT),
        name="moba",
    )(proj_t, proj_t, proj_t, proj_t)


def _tail_kernel(u_ref, v_ref, gg_ref, qm_ref, gm_ref, ymo_ref, x_ref, mk_ref, mvt_ref,
                 wst_ref, bs_ref, wot_ref, lng_ref, lnb_ref, o_ref, y_scr, *, alpha):
    tm = x_ref.shape[0]
    ch = TAIL_CHUNK
    nsub = ch // GMLP_CHUNK
    hd = HEAD_DIM
    rr = lax.broadcasted_iota(jnp.int32, (GMLP_CHUNK, GMLP_CHUNK), 0)
    cc = lax.broadcasted_iota(jnp.int32, (GMLP_CHUNK, GMLP_CHUNK), 1)

    for c in range(tm // ch):
        cs = slice(c * ch, (c + 1) * ch)
        y_scr[0:MOBA_W, cs] = ymo_ref[0, :, cs]

        for g in range(GMLP_GROUPS):
            rows = slice(g * hd, (g + 1) * hd)
            w_t = jnp.where(rr <= cc, wst_ref[g], 0.0).astype(BF16)
            vg = v_ref[0, rows, cs]
            stacked = jnp.concatenate(
                [vg[:, n * GMLP_CHUNK:(n + 1) * GMLP_CHUNK] for n in range(nsub)], axis=0)
            mixed = _dot(stacked, w_t)
            mixed_t = jnp.concatenate(
                [mixed[n * hd:(n + 1) * hd, :] for n in range(nsub)], axis=1)
            mixed_t = mixed_t + _lane_tile(bs_ref[g:g + 1, :], nsub)
            yg = u_ref[0, rows, cs].astype(F32) * mixed_t * gg_ref[0, rows, cs].astype(F32)
            y_scr[MOBA_W + g * hd:MOBA_W + (g + 1) * hd, cs] = yg.astype(BF16)

        for h in range(MEM_HEADS):
            rows = slice(h * hd, (h + 1) * hd)
            s = _dot(mk_ref[0, :, rows], qm_ref[0, rows, cs])
            m = jnp.max(s, axis=0, keepdims=True)
            p = jnp.exp2(s - m)
            l = jnp.sum(p, axis=0, keepdims=True)
            o = _dot(mvt_ref[0, rows, :], p.astype(BF16))
            ym = o / l * gm_ref[0, rows, cs].astype(F32)
            y_scr[MOBA_W + GMLP_W + h * hd:MOBA_W + GMLP_W + (h + 1) * hd, cs] = ym.astype(BF16)

        sub = _dot(wot_ref[...], y_scr[:, cs]).T
        z = alpha * x_ref[cs, :] + sub
        mu = jnp.mean(z, axis=-1, keepdims=True)
        d = z - mu
        var = jnp.mean(d * d, axis=-1, keepdims=True)
        o_ref[cs, :] = d * lax.rsqrt(var + LN_EPS) * lng_ref[...] + lnb_ref[...]


def _tail(proj_t, ymo_t, x2, mem_k, mem_vt, ws_t, b_s, wo_t, ln_g, ln_b, batch, seq, alpha):
    tm = TAIL_ROWS
    nt = seq // tm

    def proj_spec(pb):
        return pl.BlockSpec((1, PROJ_BLOCK, tm), lambda i: (i // nt, pb, i % nt))

    return pl.pallas_call(
        functools.partial(_tail_kernel, alpha=alpha),
        grid=(batch * nt,),
        in_specs=[
            proj_spec(PB_U_GM), proj_spec(PB_V_GM), proj_spec(PB_G_GM),
            proj_spec(PB_Q_ME), proj_spec(PB_G_ME),
            pl.BlockSpec((1, MOBA_W, tm), lambda i: (i // nt, 0, i % nt)),
            pl.BlockSpec((tm, D_MODEL), lambda i: (i, 0)),
            pl.BlockSpec((1, MEM_LEN, MEM_W), lambda i: (i // nt, 0, 0)),
            pl.BlockSpec((1, MEM_W, MEM_LEN), lambda i: (i // nt, 0, 0)),
            pl.BlockSpec((GMLP_GROUPS, GMLP_CHUNK, GMLP_CHUNK), lambda i: (0, 0, 0)),
            pl.BlockSpec((GMLP_GROUPS, GMLP_CHUNK), lambda i: (0, 0)),
            pl.BlockSpec((D_MODEL, D_MIX), lambda i: (0, 0), pipeline_mode=pl.Buffered(1)),
            pl.BlockSpec((1, D_MODEL), lambda i: (0, 0)),
            pl.BlockSpec((1, D_MODEL), lambda i: (0, 0)),
        ],
        out_specs=pl.BlockSpec((tm, D_MODEL), lambda i: (i, 0)),
        out_shape=jax.ShapeDtypeStruct((batch * seq, D_MODEL), F32),
        scratch_shapes=[pltpu.VMEM((D_MIX, tm), BF16)],
        compiler_params=pltpu.CompilerParams(
            dimension_semantics=("arbitrary",), vmem_limit_bytes=VMEM_LIMIT),
        name="tail",
    )(proj_t, proj_t, proj_t, proj_t, proj_t, ymo_t, x2, mem_k, mem_vt, ws_t, b_s, wo_t, ln_g, ln_b)


def kernel(x, mem, positions, w_in, w_mem_kv, gmlp_ln_g, gmlp_ln_b, gmlp_w_s, gmlp_b_s,
           w_out, ln_g, ln_b):
    batch, seq, d_model = x.shape
    depth = w_in.shape[0]
    assert d_model == D_MODEL and w_in.shape[2] == D_IN
    assert seq % INPROJ_ROWS == 0 and seq % MOBA_BLOCK == 0 and seq % TAIL_ROWS == 0
    alpha = float((2.0 * depth) ** 0.25)

    half = HEAD_DIM // 2
    inv_freq = ROPE_THETA ** (-jnp.arange(half, dtype=F32) / half)
    inv_freq = jnp.broadcast_to(inv_freq[:, None], (half, LANES))
    pos3 = positions.reshape(batch * seq // INPROJ_ROWS, 1, INPROJ_ROWS)

    x2 = x.reshape(batch * seq, d_model)
    for l in range(depth):
        w_t = w_in[l].T.astype(BF16)
        wk = w_mem_kv[l][:, :MEM_W].astype(BF16)
        wvt = w_mem_kv[l][:, MEM_W:].T.astype(BF16)
        wo_t = w_out[l].T.astype(BF16)
        ws_t = jnp.swapaxes(gmlp_w_s[l], 1, 2)
        lng = jnp.broadcast_to(gmlp_ln_g[l][:, None], (GMLP_W, LANES))
        lnb = jnp.broadcast_to(gmlp_ln_b[l][:, None], (GMLP_W, LANES))

        mem_k, mem_vt = _memkv(mem, wk, wvt)
        proj_t = _inproj(pos3, x2, w_t, inv_freq, lng, lnb, batch, seq)
        ymo_t = _moba(proj_t, batch, seq)
        x2 = _tail(proj_t, ymo_t, x2, mem_k, mem_vt, ws_t, gmlp_b_s[l], wo_t,
                   ln_g[l][None, :], ln_b[l][None, :], batch, seq, alpha)
    return x2.reshape(batch, seq, d_model)
```

```python
import functools

import jax
import jax.numpy as jnp
import numpy as np
from jax import lax
from jax.experimental import pallas as pl
from jax.experimental.pallas import tpu as pltpu

D_MODEL = 2048
HEAD_DIM = 128
MOBA_HEADS = 8
MOBA_W = MOBA_HEADS * HEAD_DIM
MOBA_BLOCK = 256
MOBA_TOPK = 3
GMLP_GROUPS = 4
GMLP_W = GMLP_GROUPS * HEAD_DIM
GMLP_CHUNK = 128
MEM_HEADS = 4
MEM_W = MEM_HEADS * HEAD_DIM
MEM_LEN = 256
D_MIX = MOBA_W + GMLP_W + MEM_W
D_IN = 4 * MOBA_W + 3 * GMLP_W + 2 * MEM_W
ROPE_THETA = 10000.0
LN_EPS = 1e-5
NEG_INF = -1e30
ATTN_SCALE = HEAD_DIM ** -0.5
LOG2E = float(np.log2(np.e))
Q_SCALE = ATTN_SCALE * LOG2E
MOBA_GROUP = 8
QK_PARTS = 2
SCORE_BUFFERS = 3
K_UNROLL = 4
GATE_TILES = 4
ONES_ROWS = 16

LANES = 128
PROJ_BLOCK = 512
N_PROJ_BLOCKS = D_IN // PROJ_BLOCK
PB_Q, PB_K, PB_V, PB_G = 0, 2, 4, 6
PB_U_GM, PB_V_GM, PB_G_GM, PB_Q_ME, PB_G_ME = 8, 9, 10, 11, 12

INPROJ_ROWS = 1024
INPROJ_CHUNK = 256
TAIL_ROWS = 512
TAIL_CHUNK = 256
VMEM_LIMIT = 48 * 1024 * 1024
MOBA_VMEM_LIMIT = 56 * 1024 * 1024

F32 = jnp.float32
BF16 = jnp.bfloat16

_NT = (((1,), (1,)), ((), ()))


def _dot(a, b):
    return jnp.dot(a, b, preferred_element_type=F32)


def _dot_nt(a, b):
    return lax.dot_general(a, b, _NT, preferred_element_type=F32)


def _lane_tile(a, reps):
    return a if reps == 1 else jnp.concatenate([a] * reps, axis=1)


def _silu(y):
    return y * (1.0 / (1.0 + jnp.exp(-y)))


def _gelu_tanh(y):
    c = np.float32(np.sqrt(2.0 / np.pi))
    return 0.5 * y * (1.0 + jnp.tanh(c * (y + 0.044715 * (y * y * y))))


def _memkv_kernel(mem_ref, wk_ref, wvt_ref, k_ref, vt_ref):
    m = mem_ref[0].astype(BF16)
    k_ref[0] = _dot(m, wk_ref[...]).astype(BF16)
    vt_ref[0] = _dot_nt(wvt_ref[...], m).astype(BF16)


def _memkv(mem, wk, wvt):
    b = mem.shape[0]
    return pl.pallas_call(
        _memkv_kernel,
        grid=(b,),
        in_specs=[
            pl.BlockSpec((1, MEM_LEN, D_MODEL), lambda i: (i, 0, 0)),
            pl.BlockSpec((D_MODEL, MEM_W), lambda i: (0, 0)),
            pl.BlockSpec((MEM_W, D_MODEL), lambda i: (0, 0)),
        ],
        out_specs=[
            pl.BlockSpec((1, MEM_LEN, MEM_W), lambda i: (i, 0, 0)),
            pl.BlockSpec((1, MEM_W, MEM_LEN), lambda i: (i, 0, 0)),
        ],
        out_shape=[
            jax.ShapeDtypeStruct((b, MEM_LEN, MEM_W), BF16),
            jax.ShapeDtypeStruct((b, MEM_W, MEM_LEN), BF16),
        ],
        compiler_params=pltpu.CompilerParams(
            dimension_semantics=("arbitrary",), vmem_limit_bytes=VMEM_LIMIT),
        name="memkv",
    )(mem, wk, wvt)


def _inproj_kernel(pos_ref, x_ref, w_ref, inv_ref, lng_ref, lnb_ref, o_ref,
                   xb_scr, cos_scr, sin_scr):
    j = pl.program_id(1)
    tm = x_ref.shape[0]

    ch = INPROJ_CHUNK
    creps = ch // LANES

    def chunked(epilogue, first=False):
        for c in range(tm // ch):
            cs = slice(c * ch, (c + 1) * ch)
            if first:
                xb_scr[cs, :] = x_ref[cs, :].astype(BF16)
                ang = _lane_tile(inv_ref[...], creps) * pos_ref[0, :, cs].astype(F32)
                cos_scr[:, cs] = jnp.cos(ang)
                sin_scr[:, cs] = jnp.sin(ang)
            y = _dot_nt(w_ref[...], xb_scr[cs, :])
            epilogue(y, cs)

    def rope_store(scale):
        def epilogue(y, cs):
            half = HEAD_DIM // 2
            c = cos_scr[:, cs]
            s = sin_scr[:, cs]
            for h in range(PROJ_BLOCK // HEAD_DIM):
                x1 = y[h * HEAD_DIM:h * HEAD_DIM + half]
                x2 = y[h * HEAD_DIM + half:(h + 1) * HEAD_DIM]
                o1 = x1 * c - x2 * s
                o2 = x2 * c + x1 * s
                if scale != 1.0:
                    o1 = o1 * scale
                    o2 = o2 * scale
                o_ref[0, h * HEAD_DIM:h * HEAD_DIM + half, cs] = o1.astype(BF16)
                o_ref[0, h * HEAD_DIM + half:(h + 1) * HEAD_DIM, cs] = o2.astype(BF16)
        return epilogue

    def store(fn):
        def epilogue(y, cs):
            o_ref[0, :, cs] = fn(y).astype(BF16)
        return epilogue

    def gelu_ln(y):
        v = _gelu_tanh(y)
        mu = jnp.mean(v, axis=0, keepdims=True)
        d = v - mu
        var = jnp.mean(d * d, axis=0, keepdims=True)
        n = d * lax.rsqrt(var + LN_EPS)
        return n * _lane_tile(lng_ref[...], creps) + _lane_tile(lnb_ref[...], creps)

    @pl.when(j == 0)
    def _():
        chunked(rope_store(Q_SCALE), first=True)

    @pl.when(jnp.logical_and(j > 0, j < PB_K))
    def _():
        chunked(rope_store(Q_SCALE))

    @pl.when(jnp.logical_and(j >= PB_K, j < PB_V))
    def _():
        chunked(rope_store(1.0))

    @pl.when(jnp.logical_and(j >= PB_V, j < PB_G))
    def _():
        chunked(store(lambda y: y))

    is_gate = jnp.logical_or(jnp.logical_and(j >= PB_G, j < PB_U_GM),
                             jnp.logical_or(j == PB_G_GM, j == PB_G_ME))

    @pl.when(is_gate)
    def _():
        chunked(store(_silu))

    @pl.when(j == PB_U_GM)
    def _():
        chunked(store(_gelu_tanh))

    @pl.when(j == PB_V_GM)
    def _():
        chunked(store(gelu_ln))

    @pl.when(j == PB_Q_ME)
    def _():
        chunked(store(lambda y: y * Q_SCALE))


def _inproj(pos3, x2, w_t, inv_freq, lng, lnb, batch, seq):
    tm = INPROJ_ROWS
    nt = seq // tm
    return pl.pallas_call(
        _inproj_kernel,
        grid=(batch * nt, N_PROJ_BLOCKS),
        in_specs=[
            pl.BlockSpec((1, 1, tm), lambda i, j: (i, 0, 0)),
            pl.BlockSpec((tm, D_MODEL), lambda i, j: (i, 0)),
            pl.BlockSpec((PROJ_BLOCK, D_MODEL), lambda i, j: (j, 0)),
            pl.BlockSpec((HEAD_DIM // 2, LANES), lambda i, j: (0, 0)),
            pl.BlockSpec((GMLP_W, LANES), lambda i, j: (0, 0)),
            pl.BlockSpec((GMLP_W, LANES), lambda i, j: (0, 0)),
        ],
        out_specs=pl.BlockSpec((1, PROJ_BLOCK, tm), lambda i, j: (i // nt, j, i % nt)),
        out_shape=jax.ShapeDtypeStruct((batch, D_IN, seq), BF16),
        scratch_shapes=[
            pltpu.VMEM((tm, D_MODEL), BF16),
            pltpu.VMEM((HEAD_DIM // 2, tm), F32),
            pltpu.VMEM((HEAD_DIM // 2, tm), F32),
        ],
        compiler_params=pltpu.CompilerParams(
            dimension_semantics=("arbitrary", "arbitrary"), vmem_limit_bytes=VMEM_LIMIT),
        name="inproj",
    )(pos3, x2, w_t, inv_freq, lng, lnb)


def _moba_kernel(q_ref, kt_ref, vt_ref, g_ref, o_ref, k_scr, km_scr, bias_scr, va_scr,
                 *s_bufs, n_steps):
    seq = kt_ref.shape[2]
    blk = MOBA_BLOCK
    nb = seq // blk
    gk = MOBA_GROUP * blk

    col = lax.broadcasted_iota(jnp.int32, (blk, LANES), 1)

    def k_body(c4, carry):
        for r in range(K_UNROLL):
            c = c4 * K_UNROLL + r
            start = pl.multiple_of(c * blk, blk)
            k = kt_ref[0, :, pl.ds(start, blk)].astype(F32).T
            k_scr[pl.ds(start, blk), 0:HEAD_DIM] = k.astype(BF16)
            k_scr[pl.ds(start, blk), HEAD_DIM:2 * HEAD_DIM] = jnp.where(col == c, 1.0, 0.0).astype(BF16)
            km_scr[pl.ds(c, 1), :] = jnp.sum(k, axis=0, keepdims=True) * (1.0 / blk)
        return carry

    lax.fori_loop(0, nb // K_UNROLL, k_body, 0)
    va_scr[0:HEAD_DIM, :] = vt_ref[0]
    va_scr[HEAD_DIM:HEAD_DIM + ONES_ROWS, :] = jnp.ones((ONES_ROWS, seq), BF16)

    km = km_scr[...]
    km_hi = km.astype(BF16)
    km_lo = (km - km_hi.astype(F32)).astype(BF16)
    gw = GATE_TILES * blk
    bid = lax.broadcasted_iota(jnp.int32, (nb, gw), 0)
    tile_in_chunk = lax.broadcasted_iota(jnp.int32, (nb, gw), 1) // blk

    def gate_body(c, carry):
        cols = pl.ds(pl.multiple_of(c * gw, gw), gw)
        qt = q_ref[0, :, cols]
        gate = _dot(km_hi, qt) + _dot(km_lo, qt)
        own_block = c * GATE_TILES + tile_in_chunk
        past = bid < own_block
        g = jnp.where(past, gate, NEG_INF)
        sel = jnp.zeros(gate.shape, F32)
        for _ in range(MOBA_TOPK):
            mx = jnp.max(g, axis=0, keepdims=True)
            first = jnp.min(jnp.where(g == mx, bid, nb), axis=0, keepdims=True)
            hit = bid == first
            sel = jnp.where(jnp.logical_and(hit, past), 1.0, sel)
            g = jnp.where(hit, -jnp.inf, g)
        bias = jnp.where(jnp.logical_or(sel > 0.0, bid == own_block), 0.0, NEG_INF)
        bias_scr[:, cols] = bias.astype(BF16)
        return carry

    lax.fori_loop(0, nb // GATE_TILES, gate_body, 0)

    def next_step(tile, pos):
        last = pos >= tile // MOBA_GROUP
        return (jnp.where(last, jnp.minimum(tile + 1, nb - 1), tile),
                jnp.where(last, 0, pos + 1))

    def group_start(tile, pos):
        g = jnp.where(pos == 0, tile // MOBA_GROUP, pos - 1)
        return pl.multiple_of(g * gk, gk)

    def tile_cols(tile):
        return pl.ds(pl.multiple_of(tile * blk, blk), blk)

    def score(step, s_ref):
        tile, pos = step
        cols = tile_cols(tile)
        parts = [q_ref[0, :, cols], bias_scr[:, cols]]
        if nb < HEAD_DIM:
            parts.append(jnp.zeros((HEAD_DIM - nb, blk), BF16))
        qa = jnp.concatenate(parts, axis=0)
        start = group_start(tile, pos)
        part = gk // QK_PARTS
        for r in range(QK_PARTS):
            s_ref[r * part:(r + 1) * part, :] = _dot(k_scr[pl.ds(start + r * part, part), :], qa)

    kk = lax.broadcasted_iota(jnp.int32, (blk, blk), 0)
    qq = lax.broadcasted_iota(jnp.int32, (blk, blk), 1)

    def absorb(step, s_ref, m, acc):
        tile, pos = step
        own = pl.ds(pl.multiple_of((tile % MOBA_GROUP) * blk, blk), blk)
        keep = kk <= qq + jnp.where(pos == 0, 0, blk)
        s_ref[own, :] = jnp.where(keep, s_ref[own, :], NEG_INF)
        s = s_ref[...]
        m_prev = jnp.minimum(m, jnp.where(pos == 0, NEG_INF, -NEG_INF).astype(F32))
        m_new = jnp.maximum(m_prev, jnp.max(s, axis=0, keepdims=True))
        p = jnp.exp2((s - m_new).astype(BF16))
        start = group_start(tile, pos)
        acc = jnp.exp2(m_prev - m_new) * acc + _dot(va_scr[:, pl.ds(start, gk)], p)
        cols = tile_cols(tile)
        l = acc[HEAD_DIM:HEAD_DIM + 1, :]
        o_ref[0, :, cols] = (acc[0:HEAD_DIM, :] / l * g_ref[0, :, cols].astype(F32)).astype(BF16)
        return m_new, acc

    zero = jnp.int32(0)
    nbuf = len(s_bufs)
    ahead = [(zero, zero)]
    for _ in range(nbuf - 1):
        ahead.append(next_step(*ahead[-1]))
    for r in range(nbuf - 1):
        score(ahead[r], s_bufs[r])

    def rounds(u, carry):
        steps = [(carry[2 * r], carry[2 * r + 1]) for r in range(nbuf)]
        m, acc = carry[2 * nbuf:]
        for r in range(nbuf):
            score(steps[-1], s_bufs[(r + nbuf - 1) % nbuf])
            m, acc = absorb(steps[0], s_bufs[r], m, acc)
            steps = steps[1:] + [next_step(*steps[-1])]
        return tuple(x for st in steps for x in st) + (m, acc)

    init = tuple(x for st in ahead for x in st) + (
        jnp.full((1, blk), NEG_INF, F32), jnp.zeros((HEAD_DIM + ONES_ROWS, blk), F32))
    lax.fori_loop(0, n_steps // nbuf, rounds, init)


def _moba(proj_t, batch, seq):
    nb = seq // MOBA_BLOCK
    assert nb <= HEAD_DIM and nb % 16 == 0, "mask rows sit beside the head dim in one 256-deep contraction"
    n_steps = sum(i // MOBA_GROUP + 1 for i in range(nb))
    assert n_steps % SCORE_BUFFERS == 0 and nb % MOBA_GROUP == 0
    hb = PROJ_BLOCK // HEAD_DIM

    def head_spec(pb):
        return pl.BlockSpec((1, HEAD_DIM, seq), lambda b, h: (b, pb * hb + h, 0),
                            pipeline_mode=pl.Buffered(1))

    gk = MOBA_GROUP * MOBA_BLOCK
    return pl.pallas_call(
        functools.partial(_moba_kernel, n_steps=n_steps),
        grid=(batch, MOBA_HEADS),
        in_specs=[head_spec(PB_Q), head_spec(PB_K), head_spec(PB_V), head_spec(PB_G)],
        out_specs=pl.BlockSpec((1, HEAD_DIM, seq), lambda b, h: (b, h, 0)),
        out_shape=jax.ShapeDtypeStruct((batch, MOBA_W, seq), BF16),
        scratch_shapes=[
            pltpu.VMEM((seq, 2 * HEAD_DIM), BF16),
            pltpu.VMEM((nb, HEAD_DIM), F32),
            pltpu.VMEM((nb, seq), BF16),
            pltpu.VMEM((HEAD_DIM + ONES_ROWS, seq), BF16),
        ] + [pltpu.VMEM((gk, MOBA_BLOCK), F32)] * SCORE_BUFFERS,
        compiler_params=pltpu.CompilerParams(
            dimension_semantics=("arbitrary", "arbitrary"),
            vmem_limit_bytes=MOBA_VMEM_LIMIT),
        name="moba",
    )(proj_t, proj_t, proj_t, proj_t)


def _tail_kernel(u_ref, v_ref, gg_ref, qm_ref, gm_ref, ymo_ref, x_ref, mk_ref, mvt_ref,
                 wst_ref, bs_ref, wot_ref, lng_ref, lnb_ref, o_ref, y_scr, *, alpha):
    tm = x_ref.shape[0]
    ch = TAIL_CHUNK
    nsub = ch // GMLP_CHUNK
    hd = HEAD_DIM
    rr = lax.broadcasted_iota(jnp.int32, (GMLP_CHUNK, GMLP_CHUNK), 0)
    cc = lax.broadcasted_iota(jnp.int32, (GMLP_CHUNK, GMLP_CHUNK), 1)

    for c in range(tm // ch):
        cs = slice(c * ch, (c + 1) * ch)
        y_scr[0:MOBA_W, cs] = ymo_ref[0, :, cs]

        for g in range(GMLP_GROUPS):
            rows = slice(g * hd, (g + 1) * hd)
            w_t = jnp.where(rr <= cc, wst_ref[g], 0.0).astype(BF16)
            vg = v_ref[0, rows, cs]
            stacked = jnp.concatenate(
                [vg[:, n * GMLP_CHUNK:(n + 1) * GMLP_CHUNK] for n in range(nsub)], axis=0)
            mixed = _dot(stacked, w_t)
            mixed_t = jnp.concatenate(
                [mixed[n * hd:(n + 1) * hd, :] for n in range(nsub)], axis=1)
            mixed_t = mixed_t + _lane_tile(bs_ref[g:g + 1, :], nsub)
            yg = u_ref[0, rows, cs].astype(F32) * mixed_t * gg_ref[0, rows, cs].astype(F32)
            y_scr[MOBA_W + g * hd:MOBA_W + (g + 1) * hd, cs] = yg.astype(BF16)

        for h in range(MEM_HEADS):
            rows = slice(h * hd, (h + 1) * hd)
            s = _dot(mk_ref[0, :, rows], qm_ref[0, rows, cs])
            m = jnp.max(s, axis=0, keepdims=True)
            p = jnp.exp2(s - m)
            l = jnp.sum(p, axis=0, keepdims=True)
            o = _dot(mvt_ref[0, rows, :], p.astype(BF16))
            ym = o / l * gm_ref[0, rows, cs].astype(F32)
            y_scr[MOBA_W + GMLP_W + h * hd:MOBA_W + GMLP_W + (h + 1) * hd, cs] = ym.astype(BF16)

        sub = _dot(wot_ref[...], y_scr[:, cs]).T
        z = alpha * x_ref[cs, :] + sub
        mu = jnp.mean(z, axis=-1, keepdims=True)
        d = z - mu
        var = jnp.mean(d * d, axis=-1, keepdims=True)
        o_ref[cs, :] = d * lax.rsqrt(var + LN_EPS) * lng_ref[...] + lnb_ref[...]


def _tail(proj_t, ymo_t, x2, mem_k, mem_vt, ws_t, b_s, wo_t, ln_g, ln_b, batch, seq, alpha):
    tm = TAIL_ROWS
    nt = seq // tm

    def proj_spec(pb):
        return pl.BlockSpec((1, PROJ_BLOCK, tm), lambda i: (i // nt, pb, i % nt))

    return pl.pallas_call(
        functools.partial(_tail_kernel, alpha=alpha),
        grid=(batch * nt,),
        in_specs=[
            proj_spec(PB_U_GM), proj_spec(PB_V_GM), proj_spec(PB_G_GM),
            proj_spec(PB_Q_ME), proj_spec(PB_G_ME),
            pl.BlockSpec((1, MOBA_W, tm), lambda i: (i // nt, 0, i % nt)),
            pl.BlockSpec((tm, D_MODEL), lambda i: (i, 0)),
            pl.BlockSpec((1, MEM_LEN, MEM_W), lambda i: (i // nt, 0, 0)),
            pl.BlockSpec((1, MEM_W, MEM_LEN), lambda i: (i // nt, 0, 0)),
            pl.BlockSpec((GMLP_GROUPS, GMLP_CHUNK, GMLP_CHUNK), lambda i: (0, 0, 0)),
            pl.BlockSpec((GMLP_GROUPS, GMLP_CHUNK), lambda i: (0, 0)),
            pl.BlockSpec((D_MODEL, D_MIX), lambda i: (0, 0), pipeline_mode=pl.Buffered(1)),
            pl.BlockSpec((1, D_MODEL), lambda i: (0, 0)),
            pl.BlockSpec((1, D_MODEL), lambda i: (0, 0)),
        ],
        out_specs=pl.BlockSpec((tm, D_MODEL), lambda i: (i, 0)),
        out_shape=jax.ShapeDtypeStruct((batch * seq, D_MODEL), F32),
        scratch_shapes=[pltpu.VMEM((D_MIX, tm), BF16)],
        compiler_params=pltpu.CompilerParams(
            dimension_semantics=("arbitrary",), vmem_limit_bytes=VMEM_LIMIT),
        name="tail",
    )(proj_t, proj_t, proj_t, proj_t, proj_t, ymo_t, x2, mem_k, mem_vt, ws_t, b_s, wo_t, ln_g, ln_b)


def kernel(x, mem, positions, w_in, w_mem_kv, gmlp_ln_g, gmlp_ln_b, gmlp_w_s, gmlp_b_s,
           w_out, ln_g, ln_b):
    batch, seq, d_model = x.shape
    depth = w_in.shape[0]
    assert d_model == D_MODEL and w_in.shape[2] == D_IN
    assert seq % INPROJ_ROWS == 0 and seq % MOBA_BLOCK == 0 and seq % TAIL_ROWS == 0
    alpha = float((2.0 * depth) ** 0.25)

    half = HEAD_DIM // 2
    inv_freq = ROPE_THETA ** (-jnp.arange(half, dtype=F32) / half)
    inv_freq = jnp.broadcast_to(inv_freq[:, None], (half, LANES))
    pos3 = positions.reshape(batch * seq // INPROJ_ROWS, 1, INPROJ_ROWS)

    x2 = x.reshape(batch * seq, d_model)
    for l in range(depth):
        w_t = w_in[l].T.astype(BF16)
        wk = w_mem_kv[l][:, :MEM_W].astype(BF16)
        wvt = w_mem_kv[l][:, MEM_W:].T.astype(BF16)
        wo_t = w_out[l].T.astype(BF16)
        ws_t = jnp.swapaxes(gmlp_w_s[l], 1, 2)
        lng = jnp.broadcast_to(gmlp_ln_g[l][:, None], (GMLP_W, LANES))
        lnb = jnp.broadcast_to(gmlp_ln_b[l][:, None], (GMLP_W, LANES))

        mem_k, mem_vt = _memkv(mem, wk, wvt)
        proj_t = _inproj(pos3, x2, w_t, inv_freq, lng, lnb, batch, seq)
        ymo_t = _moba(proj_t, batch, seq)
        x2 = _tail(proj_t, ymo_t, x2, mem_k, mem_vt, ws_t, gmlp_b_s[l], wo_t,
                   ln_g[l][None, :], ln_b[l][None, :], batch, seq, alpha)
    return x2.reshape(batch, seq, d_model)
```

```python
import functools

import jax
import jax.numpy as jnp
import numpy as np
from jax import lax
from jax.experimental import pallas as pl
from jax.experimental.pallas import tpu as pltpu

D_MODEL = 2048
HEAD_DIM = 128
MOBA_HEADS = 8
MOBA_W = MOBA_HEADS * HEAD_DIM
MOBA_BLOCK = 256
MOBA_TOPK = 3
GMLP_GROUPS = 4
GMLP_W = GMLP_GROUPS * HEAD_DIM
GMLP_CHUNK = 128
MEM_HEADS = 4
MEM_W = MEM_HEADS * HEAD_DIM
MEM_LEN = 256
D_MIX = MOBA_W + GMLP_W + MEM_W
D_IN = 4 * MOBA_W + 3 * GMLP_W + 2 * MEM_W
ROPE_THETA = 10000.0
LN_EPS = 1e-5
NEG_INF = -1e30
ATTN_SCALE = HEAD_DIM ** -0.5
LOG2E = float(np.log2(np.e))
Q_SCALE = ATTN_SCALE * LOG2E
MOBA_GROUP = 8
QK_PARTS = 2
SCORE_BUFFERS = 3
K_UNROLL = 4
GATE_TILES = 4
ONES_ROWS = 16

LANES = 128
PROJ_BLOCK = 512
N_PROJ_BLOCKS = D_IN // PROJ_BLOCK
PB_Q, PB_K, PB_V, PB_G = 0, 2, 4, 6
PB_U_GM, PB_V_GM, PB_G_GM, PB_Q_ME, PB_G_ME = 8, 9, 10, 11, 12

INPROJ_ROWS = 1024
INPROJ_CHUNK = 256
INPROJ_BUFFERS = 2
TAIL_ROWS = 512
TAIL_CHUNK = 256
VMEM_LIMIT = 48 * 1024 * 1024
MOBA_VMEM_LIMIT = 56 * 1024 * 1024

F32 = jnp.float32
BF16 = jnp.bfloat16

_NT = (((1,), (1,)), ((), ()))


def _dot(a, b):
    return jnp.dot(a, b, preferred_element_type=F32)


def _dot_nt(a, b):
    return lax.dot_general(a, b, _NT, preferred_element_type=F32)


def _lane_tile(a, reps):
    return a if reps == 1 else jnp.concatenate([a] * reps, axis=1)


def _silu(y):
    return y * (1.0 / (1.0 + jnp.exp(-y)))


def _gelu_tanh(y):
    c = np.float32(np.sqrt(2.0 / np.pi))
    return 0.5 * y * (1.0 + jnp.tanh(c * (y + 0.044715 * (y * y * y))))


def _memkv_kernel(mem_ref, wk_ref, wvt_ref, k_ref, vt_ref):
    m = mem_ref[0].astype(BF16)
    k_ref[0] = _dot(m, wk_ref[...]).astype(BF16)
    vt_ref[0] = _dot_nt(wvt_ref[...], m).astype(BF16)


def _memkv(mem, wk, wvt):
    b = mem.shape[0]
    return pl.pallas_call(
        _memkv_kernel,
        grid=(b,),
        in_specs=[
            pl.BlockSpec((1, MEM_LEN, D_MODEL), lambda i: (i, 0, 0)),
            pl.BlockSpec((D_MODEL, MEM_W), lambda i: (0, 0)),
            pl.BlockSpec((MEM_W, D_MODEL), lambda i: (0, 0)),
        ],
        out_specs=[
            pl.BlockSpec((1, MEM_LEN, MEM_W), lambda i: (i, 0, 0)),
            pl.BlockSpec((1, MEM_W, MEM_LEN), lambda i: (i, 0, 0)),
        ],
        out_shape=[
            jax.ShapeDtypeStruct((b, MEM_LEN, MEM_W), BF16),
            jax.ShapeDtypeStruct((b, MEM_W, MEM_LEN), BF16),
        ],
        compiler_params=pltpu.CompilerParams(
            dimension_semantics=("arbitrary",), vmem_limit_bytes=VMEM_LIMIT),
        name="memkv",
    )(mem, wk, wvt)


def _inproj_kernel(pos_ref, x_ref, w_ref, inv_ref, lng_ref, lnb_ref, o_ref,
                   xb_scr, cos_scr, sin_scr, *y_bufs):
    j = pl.program_id(1)
    tm = x_ref.shape[0]

    ch = INPROJ_CHUNK
    creps = ch // LANES

    def chunked(epilogue, first=False):
        n_chunks = tm // ch
        nbuf = len(y_bufs)

        def product(c):
            cs = slice(c * ch, (c + 1) * ch)
            if first:
                xb_scr[cs, :] = x_ref[cs, :].astype(BF16)
                ang = _lane_tile(inv_ref[...], creps) * pos_ref[0, :, cs].astype(F32)
                cos_scr[:, cs] = jnp.cos(ang)
                sin_scr[:, cs] = jnp.sin(ang)
            y_bufs[c % nbuf][...] = _dot_nt(w_ref[...], xb_scr[cs, :])

        for c in range(min(nbuf - 1, n_chunks)):
            product(c)
        for c in range(n_chunks):
            if c + nbuf - 1 < n_chunks:
                product(c + nbuf - 1)
            epilogue(y_bufs[c % nbuf][...], slice(c * ch, (c + 1) * ch))

    def rope_store(scale):
        def epilogue(y, cs):
            half = HEAD_DIM // 2
            c = cos_scr[:, cs]
            s = sin_scr[:, cs]
            for h in range(PROJ_BLOCK // HEAD_DIM):
                x1 = y[h * HEAD_DIM:h * HEAD_DIM + half]
                x2 = y[h * HEAD_DIM + half:(h + 1) * HEAD_DIM]
                o1 = x1 * c - x2 * s
                o2 = x2 * c + x1 * s
                if scale != 1.0:
                    o1 = o1 * scale
                    o2 = o2 * scale
                o_ref[0, h * HEAD_DIM:h * HEAD_DIM + half, cs] = o1.astype(BF16)
                o_ref[0, h * HEAD_DIM + half:(h + 1) * HEAD_DIM, cs] = o2.astype(BF16)
        return epilogue

    def store(fn):
        def epilogue(y, cs):
            o_ref[0, :, cs] = fn(y).astype(BF16)
        return epilogue

    def gelu_ln(y):
        v = _gelu_tanh(y)
        mu = jnp.mean(v, axis=0, keepdims=True)
        d = v - mu
        var = jnp.mean(d * d, axis=0, keepdims=True)
        n = d * lax.rsqrt(var + LN_EPS)
        return n * _lane_tile(lng_ref[...], creps) + _lane_tile(lnb_ref[...], creps)

    @pl.when(j == 0)
    def _():
        chunked(rope_store(Q_SCALE), first=True)

    @pl.when(jnp.logical_and(j > 0, j < PB_K))
    def _():
        chunked(rope_store(Q_SCALE))

    @pl.when(jnp.logical_and(j >= PB_K, j < PB_V))
    def _():
        chunked(rope_store(1.0))

    @pl.when(jnp.logical_and(j >= PB_V, j < PB_G))
    def _():
        chunked(store(lambda y: y))

    is_gate = jnp.logical_or(jnp.logical_and(j >= PB_G, j < PB_U_GM),
                             jnp.logical_or(j == PB_G_GM, j == PB_G_ME))

    @pl.when(is_gate)
    def _():
        chunked(store(_silu))

    @pl.when(j == PB_U_GM)
    def _():
        chunked(store(_gelu_tanh))

    @pl.when(j == PB_V_GM)
    def _():
        chunked(store(gelu_ln))

    @pl.when(j == PB_Q_ME)
    def _():
        chunked(store(lambda y: y * Q_SCALE))


def _inproj(pos3, x2, w_t, inv_freq, lng, lnb, batch, seq):
    tm = INPROJ_ROWS
    nt = seq // tm
    return pl.pallas_call(
        _inproj_kernel,
        grid=(batch * nt, N_PROJ_BLOCKS),
        in_specs=[
            pl.BlockSpec((1, 1, tm), lambda i, j: (i, 0, 0)),
            pl.BlockSpec((tm, D_MODEL), lambda i, j: (i, 0)),
            pl.BlockSpec((PROJ_BLOCK, D_MODEL), lambda i, j: (j, 0)),
            pl.BlockSpec((HEAD_DIM // 2, LANES), lambda i, j: (0, 0)),
            pl.BlockSpec((GMLP_W, LANES), lambda i, j: (0, 0)),
            pl.BlockSpec((GMLP_W, LANES), lambda i, j: (0, 0)),
        ],
        out_specs=pl.BlockSpec((1, PROJ_BLOCK, tm), lambda i, j: (i // nt, j, i % nt)),
        out_shape=jax.ShapeDtypeStruct((batch, D_IN, seq), BF16),
        scratch_shapes=[
            pltpu.VMEM((tm, D_MODEL), BF16),
            pltpu.VMEM((HEAD_DIM // 2, tm), F32),
            pltpu.VMEM((HEAD_DIM // 2, tm), F32),
        ] + [pltpu.VMEM((PROJ_BLOCK, INPROJ_CHUNK), F32)] * INPROJ_BUFFERS,
        compiler_params=pltpu.CompilerParams(
            dimension_semantics=("arbitrary", "arbitrary"), vmem_limit_bytes=VMEM_LIMIT),
        name="inproj",
    )(pos3, x2, w_t, inv_freq, lng, lnb)


def _moba_kernel(q_ref, kt_ref, vt_ref, o_ref, k_scr, km_scr, bias_scr, va_scr,
                 *s_bufs, n_steps):
    seq = kt_ref.shape[2]
    blk = MOBA_BLOCK
    nb = seq // blk
    gk = MOBA_GROUP * blk

    col = lax.broadcasted_iota(jnp.int32, (blk, LANES), 1)

    def k_body(c4, carry):
        for r in range(K_UNROLL):
            c = c4 * K_UNROLL + r
            start = pl.multiple_of(c * blk, blk)
            k = kt_ref[0, :, pl.ds(start, blk)].astype(F32).T
            k_scr[pl.ds(start, blk), 0:HEAD_DIM] = k.astype(BF16)
            k_scr[pl.ds(start, blk), HEAD_DIM:2 * HEAD_DIM] = jnp.where(col == c, 1.0, 0.0).astype(BF16)
            km_scr[pl.ds(c, 1), :] = jnp.sum(k, axis=0, keepdims=True) * (1.0 / blk)
        return carry

    lax.fori_loop(0, nb // K_UNROLL, k_body, 0)
    va_scr[0:HEAD_DIM, :] = vt_ref[0]
    va_scr[HEAD_DIM:HEAD_DIM + ONES_ROWS, :] = jnp.ones((ONES_ROWS, seq), BF16)

    km = km_scr[...]
    km_hi = km.astype(BF16)
    km_lo = (km - km_hi.astype(F32)).astype(BF16)
    gw = GATE_TILES * blk
    bid = lax.broadcasted_iota(jnp.int32, (nb, gw), 0)
    tile_in_chunk = lax.broadcasted_iota(jnp.int32, (nb, gw), 1) // blk

    def gate_body(c, carry):
        cols = pl.ds(pl.multiple_of(c * gw, gw), gw)
        qt = q_ref[0, :, cols]
        gate = _dot(km_hi, qt) + _dot(km_lo, qt)
        own_block = c * GATE_TILES + tile_in_chunk
        past = bid < own_block
        g = jnp.where(past, gate, NEG_INF)
        sel = jnp.zeros(gate.shape, F32)
        for _ in range(MOBA_TOPK):
            mx = jnp.max(g, axis=0, keepdims=True)
            first = jnp.min(jnp.where(g == mx, bid, nb), axis=0, keepdims=True)
            hit = bid == first
            sel = jnp.where(jnp.logical_and(hit, past), 1.0, sel)
            g = jnp.where(hit, -jnp.inf, g)
        bias = jnp.where(jnp.logical_or(sel > 0.0, bid == own_block), 0.0, NEG_INF)
        bias_scr[:, cols] = bias.astype(BF16)
        return carry

    lax.fori_loop(0, nb // GATE_TILES, gate_body, 0)

    def next_step(tile, pos):
        last = pos >= tile // MOBA_GROUP
        return (jnp.where(last, jnp.minimum(tile + 1, nb - 1), tile),
                jnp.where(last, 0, pos + 1))

    def group_start(tile, pos):
        g = jnp.where(pos == 0, tile // MOBA_GROUP, pos - 1)
        return pl.multiple_of(g * gk, gk)

    def tile_cols(tile):
        return pl.ds(pl.multiple_of(tile * blk, blk), blk)

    def score(step, s_ref):
        tile, pos = step
        cols = tile_cols(tile)
        parts = [q_ref[0, :, cols], bias_scr[:, cols]]
        if nb < HEAD_DIM:
            parts.append(jnp.zeros((HEAD_DIM - nb, blk), BF16))
        qa = jnp.concatenate(parts, axis=0)
        start = group_start(tile, pos)
        part = gk // QK_PARTS
        for r in range(QK_PARTS):
            s_ref[r * part:(r + 1) * part, :] = _dot(k_scr[pl.ds(start + r * part, part), :], qa)

    kk = lax.broadcasted_iota(jnp.int32, (blk, blk), 0)
    qq = lax.broadcasted_iota(jnp.int32, (blk, blk), 1)

    def absorb(step, s_ref, m, acc):
        tile, pos = step
        own = pl.ds(pl.multiple_of((tile % MOBA_GROUP) * blk, blk), blk)
        keep = kk <= qq + jnp.where(pos == 0, 0, blk)
        s_ref[own, :] = jnp.where(keep, s_ref[own, :], NEG_INF)
        s = s_ref[...]
        m_prev = jnp.minimum(m, jnp.where(pos == 0, NEG_INF, -NEG_INF).astype(F32))
        m_new = jnp.maximum(m_prev, jnp.max(s, axis=0, keepdims=True))
        p = jnp.exp2((s - m_new).astype(BF16))
        start = group_start(tile, pos)
        acc = jnp.exp2(m_prev - m_new) * acc + _dot(va_scr[:, pl.ds(start, gk)], p)
        cols = tile_cols(tile)
        l = acc[HEAD_DIM:HEAD_DIM + 1, :]
        o_ref[0, :, cols] = (acc[0:HEAD_DIM, :] / l).astype(BF16)
        return m_new, acc

    zero = jnp.int32(0)
    nbuf = len(s_bufs)
    ahead = [(zero, zero)]
    for _ in range(nbuf - 1):
        ahead.append(next_step(*ahead[-1]))
    for r in range(nbuf - 1):
        score(ahead[r], s_bufs[r])

    def rounds(u, carry):
        steps = [(carry[2 * r], carry[2 * r + 1]) for r in range(nbuf)]
        m, acc = carry[2 * nbuf:]
        for r in range(nbuf):
            score(steps[-1], s_bufs[(r + nbuf - 1) % nbuf])
            m, acc = absorb(steps[0], s_bufs[r], m, acc)
            steps = steps[1:] + [next_step(*steps[-1])]
        return tuple(x for st in steps for x in st) + (m, acc)

    init = tuple(x for st in ahead for x in st) + (
        jnp.full((1, blk), NEG_INF, F32), jnp.zeros((HEAD_DIM + ONES_ROWS, blk), F32))
    lax.fori_loop(0, n_steps // nbuf, rounds, init)


def _moba(proj_t, batch, seq):
    nb = seq // MOBA_BLOCK
    assert nb <= HEAD_DIM and nb % 16 == 0, "mask rows sit beside the head dim in one 256-deep contraction"
    n_steps = sum(i // MOBA_GROUP + 1 for i in range(nb))
    assert n_steps % SCORE_BUFFERS == 0 and nb % MOBA_GROUP == 0
    hb = PROJ_BLOCK // HEAD_DIM

    def head_spec(pb):
        return pl.BlockSpec((1, HEAD_DIM, seq), lambda b, h: (b, pb * hb + h, 0))

    gk = MOBA_GROUP * MOBA_BLOCK
    return pl.pallas_call(
        functools.partial(_moba_kernel, n_steps=n_steps),
        grid=(batch, MOBA_HEADS),
        in_specs=[head_spec(PB_Q), head_spec(PB_K), head_spec(PB_V)],
        out_specs=pl.BlockSpec((1, HEAD_DIM, seq), lambda b, h: (b, h, 0)),
        out_shape=jax.ShapeDtypeStruct((batch, MOBA_W, seq), BF16),
        scratch_shapes=[
            pltpu.VMEM((seq, 2 * HEAD_DIM), BF16),
            pltpu.VMEM((nb, HEAD_DIM), F32),
            pltpu.VMEM((nb, seq), BF16),
            pltpu.VMEM((HEAD_DIM + ONES_ROWS, seq), BF16),
        ] + [pltpu.VMEM((gk, MOBA_BLOCK), F32)] * SCORE_BUFFERS,
        compiler_params=pltpu.CompilerParams(
            dimension_semantics=("arbitrary", "arbitrary"),
            vmem_limit_bytes=MOBA_VMEM_LIMIT),
        name="moba",
    )(proj_t, proj_t, proj_t)


def _tail_kernel(u_ref, v_ref, gg_ref, qm_ref, gm_ref, gmo_ref, ymo_ref, x_ref, mk_ref, mvt_ref,
                 wst_ref, bs_ref, wot_ref, lng_ref, lnb_ref, o_ref, y_scr, *, alpha):
    tm = x_ref.shape[0]
    ch = TAIL_CHUNK
    nsub = ch // GMLP_CHUNK
    hd = HEAD_DIM
    rr = lax.broadcasted_iota(jnp.int32, (GMLP_CHUNK, GMLP_CHUNK), 0)
    cc = lax.broadcasted_iota(jnp.int32, (GMLP_CHUNK, GMLP_CHUNK), 1)

    for c in range(tm // ch):
        cs = slice(c * ch, (c + 1) * ch)
        y_scr[0:MOBA_W, cs] = ymo_ref[0, :, cs] * gmo_ref[0, :, cs]

        for g in range(GMLP_GROUPS):
            rows = slice(g * hd, (g + 1) * hd)
            w_t = jnp.where(rr <= cc, wst_ref[g], 0.0).astype(BF16)
            vg = v_ref[0, rows, cs]
            stacked = jnp.concatenate(
                [vg[:, n * GMLP_CHUNK:(n + 1) * GMLP_CHUNK] for n in range(nsub)], axis=0)
            mixed = _dot(stacked, w_t)
            mixed_t = jnp.concatenate(
                [mixed[n * hd:(n + 1) * hd, :] for n in range(nsub)], axis=1)
            mixed_t = mixed_t + _lane_tile(bs_ref[g:g + 1, :], nsub)
            yg = u_ref[0, rows, cs].astype(F32) * mixed_t * gg_ref[0, rows, cs].astype(F32)
            y_scr[MOBA_W + g * hd:MOBA_W + (g + 1) * hd, cs] = yg.astype(BF16)

        for h in range(MEM_HEADS):
            rows = slice(h * hd, (h + 1) * hd)
            s = _dot(mk_ref[0, :, rows], qm_ref[0, rows, cs])
            m = jnp.max(s, axis=0, keepdims=True)
            p = jnp.exp2(s - m)
            l = jnp.sum(p, axis=0, keepdims=True)
            o = _dot(mvt_ref[0, rows, :], p.astype(BF16))
            ym = o / l * gm_ref[0, rows, cs].astype(F32)
            y_scr[MOBA_W + GMLP_W + h * hd:MOBA_W + GMLP_W + (h + 1) * hd, cs] = ym.astype(BF16)

        sub = _dot(wot_ref[...], y_scr[:, cs]).T
        z = alpha * x_ref[cs, :] + sub
        mu = jnp.mean(z, axis=-1, keepdims=True)
        d = z - mu
        var = jnp.mean(d * d, axis=-1, keepdims=True)
        o_ref[cs, :] = d * lax.rsqrt(var + LN_EPS) * lng_ref[...] + lnb_ref[...]


def _tail(proj_t, ymo_t, x2, mem_k, mem_vt, ws_t, b_s, wo_t, ln_g, ln_b, batch, seq, alpha):
    tm = TAIL_ROWS
    nt = seq // tm

    def proj_spec(pb):
        return pl.BlockSpec((1, PROJ_BLOCK, tm), lambda i: (i // nt, pb, i % nt))

    return pl.pallas_call(
        functools.partial(_tail_kernel, alpha=alpha),
        grid=(batch * nt,),
        in_specs=[
            proj_spec(PB_U_GM), proj_spec(PB_V_GM), proj_spec(PB_G_GM),
            proj_spec(PB_Q_ME), proj_spec(PB_G_ME),
            pl.BlockSpec((1, MOBA_W, tm), lambda i: (i // nt, PB_G * PROJ_BLOCK // MOBA_W, i % nt)),
            pl.BlockSpec((1, MOBA_W, tm), lambda i: (i // nt, 0, i % nt)),
            pl.BlockSpec((tm, D_MODEL), lambda i: (i, 0)),
            pl.BlockSpec((1, MEM_LEN, MEM_W), lambda i: (i // nt, 0, 0)),
            pl.BlockSpec((1, MEM_W, MEM_LEN), lambda i: (i // nt, 0, 0)),
            pl.BlockSpec((GMLP_GROUPS, GMLP_CHUNK, GMLP_CHUNK), lambda i: (0, 0, 0)),
            pl.BlockSpec((GMLP_GROUPS, GMLP_CHUNK), lambda i: (0, 0)),
            pl.BlockSpec((D_MODEL, D_MIX), lambda i: (0, 0), pipeline_mode=pl.Buffered(1)),
            pl.BlockSpec((1, D_MODEL), lambda i: (0, 0)),
            pl.BlockSpec((1, D_MODEL), lambda i: (0, 0)),
        ],
        out_specs=pl.BlockSpec((tm, D_MODEL), lambda i: (i, 0)),
        out_shape=jax.ShapeDtypeStruct((batch * seq, D_MODEL), F32),
        scratch_shapes=[pltpu.VMEM((D_MIX, tm), BF16)],
        compiler_params=pltpu.CompilerParams(
            dimension_semantics=("arbitrary",), vmem_limit_bytes=VMEM_LIMIT),
        name="tail",
    )(proj_t, proj_t, proj_t, proj_t, proj_t, proj_t, ymo_t, x2, mem_k, mem_vt, ws_t, b_s, wo_t,
      ln_g, ln_b)


def kernel(x, mem, positions, w_in, w_mem_kv, gmlp_ln_g, gmlp_ln_b, gmlp_w_s, gmlp_b_s,
           w_out, ln_g, ln_b):
    batch, seq, d_model = x.shape
    depth = w_in.shape[0]
    assert d_model == D_MODEL and w_in.shape[2] == D_IN
    assert seq % INPROJ_ROWS == 0 and seq % MOBA_BLOCK == 0 and seq % TAIL_ROWS == 0
    alpha = float((2.0 * depth) ** 0.25)

    half = HEAD_DIM // 2
    inv_freq = ROPE_THETA ** (-jnp.arange(half, dtype=F32) / half)
    inv_freq = jnp.broadcast_to(inv_freq[:, None], (half, LANES))
    pos3 = positions.reshape(batch * seq // INPROJ_ROWS, 1, INPROJ_ROWS)

    x2 = x.reshape(batch * seq, d_model)
    for l in range(depth):
        w_t = w_in[l].T.astype(BF16)
        wk = w_mem_kv[l][:, :MEM_W].astype(BF16)
        wvt = w_mem_kv[l][:, MEM_W:].T.astype(BF16)
        wo_t = w_out[l].T.astype(BF16)
        ws_t = jnp.swapaxes(gmlp_w_s[l], 1, 2)
        lng = jnp.broadcast_to(gmlp_ln_g[l][:, None], (GMLP_W, LANES))
        lnb = jnp.broadcast_to(gmlp_ln_b[l][:, None], (GMLP_W, LANES))

        mem_k, mem_vt = _memkv(mem, wk, wvt)
        proj_t = _inproj(pos3, x2, w_t, inv_freq, lng, lnb, batch, seq)
        ymo_t = _moba(proj_t, batch, seq)
        x2 = _tail(proj_t, ymo_t, x2, mem_k, mem_vt, ws_t, gmlp_b_s[l], wo_t,
                   ln_g[l][None, :], ln_b[l][None, :], batch, seq, alpha)
    return x2.reshape(batch, seq, d_model)
```

```python
import functools

import jax
import jax.numpy as jnp
import numpy as np
from jax import lax
from jax.experimental import pallas as pl
from jax.experimental.pallas import tpu as pltpu

D_MODEL = 2048
HEAD_DIM = 128
MOBA_HEADS = 8
MOBA_W = MOBA_HEADS * HEAD_DIM
MOBA_BLOCK = 256
MOBA_TOPK = 3
GMLP_GROUPS = 4
GMLP_W = GMLP_GROUPS * HEAD_DIM
GMLP_CHUNK = 128
MEM_HEADS = 4
MEM_W = MEM_HEADS * HEAD_DIM
MEM_LEN = 256
D_MIX = MOBA_W + GMLP_W + MEM_W
D_IN = 4 * MOBA_W + 3 * GMLP_W + 2 * MEM_W
ROPE_THETA = 10000.0
LN_EPS = 1e-5
NEG_INF = -1e30
ATTN_SCALE = HEAD_DIM ** -0.5
LOG2E = float(np.log2(np.e))
Q_SCALE = ATTN_SCALE * LOG2E
MOBA_GROUP = 8
QK_PARTS = 2
SCORE_BUFFERS = 3
K_UNROLL = 8
GATE_TILES = 8
ONES_ROWS = 16

LANES = 128
PROJ_BLOCK = 512
N_PROJ_BLOCKS = D_IN // PROJ_BLOCK
PB_Q, PB_K, PB_V, PB_G = 0, 2, 4, 6
PB_U_GM, PB_V_GM, PB_G_GM, PB_Q_ME, PB_G_ME = 8, 9, 10, 11, 12

INPROJ_ROWS = 1024
INPROJ_CHUNK = 256
INPROJ_BUFFERS = 2
TAIL_ROWS = 512
TAIL_CHUNK = 256
VMEM_LIMIT = 48 * 1024 * 1024
MOBA_VMEM_LIMIT = 56 * 1024 * 1024

F32 = jnp.float32
BF16 = jnp.bfloat16

_NT = (((1,), (1,)), ((), ()))


def _dot(a, b):
    return jnp.dot(a, b, preferred_element_type=F32)


def _dot_nt(a, b):
    return lax.dot_general(a, b, _NT, preferred_element_type=F32)


def _lane_tile(a, reps):
    return a if reps == 1 else jnp.concatenate([a] * reps, axis=1)


def _silu(y):
    return y * (1.0 / (1.0 + jnp.exp(-y)))


def _gelu_tanh(y):
    c = np.float32(np.sqrt(2.0 / np.pi))
    return 0.5 * y * (1.0 + jnp.tanh(c * (y + 0.044715 * (y * y * y))))


def _memkv_kernel(mem_ref, wk_ref, wvt_ref, k_ref, vt_ref):
    m = mem_ref[0].astype(BF16)
    k_ref[0] = _dot(m, wk_ref[...]).astype(BF16)
    vt_ref[0] = _dot_nt(wvt_ref[...], m).astype(BF16)


def _memkv(mem, wk, wvt):
    b = mem.shape[0]
    return pl.pallas_call(
        _memkv_kernel,
        grid=(b,),
        in_specs=[
            pl.BlockSpec((1, MEM_LEN, D_MODEL), lambda i: (i, 0, 0)),
            pl.BlockSpec((D_MODEL, MEM_W), lambda i: (0, 0)),
            pl.BlockSpec((MEM_W, D_MODEL), lambda i: (0, 0)),
        ],
        out_specs=[
            pl.BlockSpec((1, MEM_LEN, MEM_W), lambda i: (i, 0, 0)),
            pl.BlockSpec((1, MEM_W, MEM_LEN), lambda i: (i, 0, 0)),
        ],
        out_shape=[
            jax.ShapeDtypeStruct((b, MEM_LEN, MEM_W), BF16),
            jax.ShapeDtypeStruct((b, MEM_W, MEM_LEN), BF16),
        ],
        compiler_params=pltpu.CompilerParams(
            dimension_semantics=("arbitrary",), vmem_limit_bytes=VMEM_LIMIT),
        name="memkv",
    )(mem, wk, wvt)


def _inproj_kernel(pos_ref, x_ref, w_ref, inv_ref, lng_ref, lnb_ref, o_ref,
                   xb_scr, cos_scr, sin_scr, *y_bufs):
    j = pl.program_id(1)
    tm = x_ref.shape[0]

    ch = INPROJ_CHUNK
    creps = ch // LANES

    def chunked(epilogue, first=False):
        n_chunks = tm // ch
        nbuf = len(y_bufs)

        def product(c):
            cs = slice(c * ch, (c + 1) * ch)
            if first:
                xb_scr[cs, :] = x_ref[cs, :].astype(BF16)
                ang = _lane_tile(inv_ref[...], creps) * pos_ref[0, :, cs].astype(F32)
                cos_scr[:, cs] = jnp.cos(ang)
                sin_scr[:, cs] = jnp.sin(ang)
            y_bufs[c % nbuf][...] = _dot_nt(w_ref[...], xb_scr[cs, :])

        for c in range(min(nbuf - 1, n_chunks)):
            product(c)
        for c in range(n_chunks):
            if c + nbuf - 1 < n_chunks:
                product(c + nbuf - 1)
            epilogue(y_bufs[c % nbuf][...], slice(c * ch, (c + 1) * ch))

    def rope_store(scale):
        def epilogue(y, cs):
            half = HEAD_DIM // 2
            c = cos_scr[:, cs]
            s = sin_scr[:, cs]
            for h in range(PROJ_BLOCK // HEAD_DIM):
                x1 = y[h * HEAD_DIM:h * HEAD_DIM + half]
                x2 = y[h * HEAD_DIM + half:(h + 1) * HEAD_DIM]
                o1 = x1 * c - x2 * s
                o2 = x2 * c + x1 * s
                if scale != 1.0:
                    o1 = o1 * scale
                    o2 = o2 * scale
                o_ref[0, h * HEAD_DIM:h * HEAD_DIM + half, cs] = o1.astype(BF16)
                o_ref[0, h * HEAD_DIM + half:(h + 1) * HEAD_DIM, cs] = o2.astype(BF16)
        return epilogue

    def store(fn):
        def epilogue(y, cs):
            o_ref[0, :, cs] = fn(y).astype(BF16)
        return epilogue

    def gelu_ln(y):
        v = _gelu_tanh(y)
        mu = jnp.mean(v, axis=0, keepdims=True)
        d = v - mu
        var = jnp.mean(d * d, axis=0, keepdims=True)
        n = d * lax.rsqrt(var + LN_EPS)
        return n * _lane_tile(lng_ref[...], creps) + _lane_tile(lnb_ref[...], creps)

    @pl.when(j == 0)
    def _():
        chunked(rope_store(Q_SCALE), first=True)

    @pl.when(jnp.logical_and(j > 0, j < PB_K))
    def _():
        chunked(rope_store(Q_SCALE))

    @pl.when(jnp.logical_and(j >= PB_K, j < PB_V))
    def _():
        chunked(rope_store(1.0))

    @pl.when(jnp.logical_and(j >= PB_V, j < PB_G))
    def _():
        chunked(store(lambda y: y))

    is_gate = jnp.logical_or(jnp.logical_and(j >= PB_G, j < PB_U_GM),
                             jnp.logical_or(j == PB_G_GM, j == PB_G_ME))

    @pl.when(is_gate)
    def _():
        chunked(store(_silu))

    @pl.when(j == PB_U_GM)
    def _():
        chunked(store(_gelu_tanh))

    @pl.when(j == PB_V_GM)
    def _():
        chunked(store(gelu_ln))

    @pl.when(j == PB_Q_ME)
    def _():
        chunked(store(lambda y: y * Q_SCALE))


def _inproj(pos3, x2, w_t, inv_freq, lng, lnb, batch, seq):
    tm = INPROJ_ROWS
    nt = seq // tm
    return pl.pallas_call(
        _inproj_kernel,
        grid=(batch * nt, N_PROJ_BLOCKS),
        in_specs=[
            pl.BlockSpec((1, 1, tm), lambda i, j: (i, 0, 0)),
            pl.BlockSpec((tm, D_MODEL), lambda i, j: (i, 0)),
            pl.BlockSpec((PROJ_BLOCK, D_MODEL), lambda i, j: (j, 0)),
            pl.BlockSpec((HEAD_DIM // 2, LANES), lambda i, j: (0, 0)),
            pl.BlockSpec((GMLP_W, LANES), lambda i, j: (0, 0)),
            pl.BlockSpec((GMLP_W, LANES), lambda i, j: (0, 0)),
        ],
        out_specs=pl.BlockSpec((1, PROJ_BLOCK, tm), lambda i, j: (i // nt, j, i % nt)),
        out_shape=jax.ShapeDtypeStruct((batch, D_IN, seq), BF16),
        scratch_shapes=[
            pltpu.VMEM((tm, D_MODEL), BF16),
            pltpu.VMEM((HEAD_DIM // 2, tm), F32),
            pltpu.VMEM((HEAD_DIM // 2, tm), F32),
        ] + [pltpu.VMEM((PROJ_BLOCK, INPROJ_CHUNK), F32)] * INPROJ_BUFFERS,
        compiler_params=pltpu.CompilerParams(
            dimension_semantics=("arbitrary", "arbitrary"), vmem_limit_bytes=VMEM_LIMIT),
        name="inproj",
    )(pos3, x2, w_t, inv_freq, lng, lnb)


def _moba_kernel(q_ref, kt_ref, vt_ref, o_ref, k_scr, km_scr, bias_scr, va_scr,
                 *s_bufs, n_steps):
    seq = kt_ref.shape[2]
    blk = MOBA_BLOCK
    nb = seq // blk
    gk = MOBA_GROUP * blk

    col = lax.broadcasted_iota(jnp.int32, (blk, LANES), 1)

    def k_body(c4, carry):
        for r in range(K_UNROLL):
            c = c4 * K_UNROLL + r
            start = pl.multiple_of(c * blk, blk)
            k = kt_ref[0, :, pl.ds(start, blk)].astype(F32).T
            k_scr[pl.ds(start, blk), 0:HEAD_DIM] = k.astype(BF16)
            k_scr[pl.ds(start, blk), HEAD_DIM:2 * HEAD_DIM] = jnp.where(col == c, 1.0, 0.0).astype(BF16)
            km_scr[pl.ds(c, 1), :] = jnp.sum(k, axis=0, keepdims=True) * (1.0 / blk)
        return carry

    lax.fori_loop(0, nb // K_UNROLL, k_body, 0)
    va_scr[0:HEAD_DIM, :] = vt_ref[0]
    va_scr[HEAD_DIM:HEAD_DIM + ONES_ROWS, :] = jnp.ones((ONES_ROWS, seq), BF16)

    km = km_scr[...]
    km_hi = km.astype(BF16)
    km_lo = (km - km_hi.astype(F32)).astype(BF16)
    gw = GATE_TILES * blk
    bid = lax.broadcasted_iota(jnp.int32, (nb, gw), 0)
    tile_in_chunk = lax.broadcasted_iota(jnp.int32, (nb, gw), 1) // blk

    def gate_body(c, carry):
        cols = pl.ds(pl.multiple_of(c * gw, gw), gw)
        qt = q_ref[0, :, cols]
        gate = _dot(km_hi, qt) + _dot(km_lo, qt)
        own_block = c * GATE_TILES + tile_in_chunk
        past = bid < own_block
        g = jnp.where(past, gate, NEG_INF)
        sel = jnp.zeros(gate.shape, F32)
        for _ in range(MOBA_TOPK):
            mx = jnp.max(g, axis=0, keepdims=True)
            first = jnp.min(jnp.where(g == mx, bid, nb), axis=0, keepdims=True)
            hit = bid == first
            sel = jnp.where(jnp.logical_and(hit, past), 1.0, sel)
            g = jnp.where(hit, -jnp.inf, g)
        bias = jnp.where(jnp.logical_or(sel > 0.0, bid == own_block), 0.0, NEG_INF)
        bias_scr[:, cols] = bias.astype(BF16)
        return carry

    lax.fori_loop(0, nb // GATE_TILES, gate_body, 0)

    def next_step(tile, pos):
        last = pos >= tile // MOBA_GROUP
        return (jnp.where(last, jnp.minimum(tile + 1, nb - 1), tile),
                jnp.where(last, 0, pos + 1))

    def group_start(tile, pos):
        g = jnp.where(pos == 0, tile // MOBA_GROUP, pos - 1)
        return pl.multiple_of(g * gk, gk)

    def tile_cols(tile):
        return pl.ds(pl.multiple_of(tile * blk, blk), blk)

    def score(step, s_ref):
        tile, pos = step
        cols = tile_cols(tile)
        parts = [q_ref[0, :, cols], bias_scr[:, cols]]
        if nb < HEAD_DIM:
            parts.append(jnp.zeros((HEAD_DIM - nb, blk), BF16))
        qa = jnp.concatenate(parts, axis=0)
        start = group_start(tile, pos)
        part = gk // QK_PARTS
        for r in range(QK_PARTS):
            s_ref[r * part:(r + 1) * part, :] = _dot(k_scr[pl.ds(start + r * part, part), :], qa)

    kk = lax.broadcasted_iota(jnp.int32, (blk, blk), 0)
    qq = lax.broadcasted_iota(jnp.int32, (blk, blk), 1)

    def absorb(step, s_ref, m, acc):
        tile, pos = step
        own = pl.ds(pl.multiple_of((tile % MOBA_GROUP) * blk, blk), blk)
        keep = kk <= qq + jnp.where(pos == 0, 0, blk)
        s_ref[own, :] = jnp.where(keep, s_ref[own, :], NEG_INF)
        s = s_ref[...]
        m_prev = jnp.minimum(m, jnp.where(pos == 0, NEG_INF, -NEG_INF).astype(F32))
        m_new = jnp.maximum(m_prev, jnp.max(s, axis=0, keepdims=True))
        p = jnp.exp2((s - m_new).astype(BF16))
        start = group_start(tile, pos)
        acc = jnp.exp2(m_prev - m_new) * acc + _dot(va_scr[:, pl.ds(start, gk)], p)
        cols = tile_cols(tile)
        l = acc[HEAD_DIM:HEAD_DIM + 1, :]
        o_ref[0, :, cols] = (acc[0:HEAD_DIM, :] / l).astype(BF16)
        return m_new, acc

    zero = jnp.int32(0)
    nbuf = len(s_bufs)
    ahead = [(zero, zero)]
    for _ in range(nbuf - 1):
        ahead.append(next_step(*ahead[-1]))
    for r in range(nbuf - 1):
        score(ahead[r], s_bufs[r])

    def rounds(u, carry):
        steps = [(carry[2 * r], carry[2 * r + 1]) for r in range(nbuf)]
        m, acc = carry[2 * nbuf:]
        for r in range(nbuf):
            score(steps[-1], s_bufs[(r + nbuf - 1) % nbuf])
            m, acc = absorb(steps[0], s_bufs[r], m, acc)
            steps = steps[1:] + [next_step(*steps[-1])]
        return tuple(x for st in steps for x in st) + (m, acc)

    init = tuple(x for st in ahead for x in st) + (
        jnp.full((1, blk), NEG_INF, F32), jnp.zeros((HEAD_DIM + ONES_ROWS, blk), F32))
    lax.fori_loop(0, n_steps // nbuf, rounds, init)


def _moba(proj_t, batch, seq):
    nb = seq // MOBA_BLOCK
    assert nb <= HEAD_DIM and nb % 16 == 0, "mask rows sit beside the head dim in one 256-deep contraction"
    n_steps = sum(i // MOBA_GROUP + 1 for i in range(nb))
    assert n_steps % SCORE_BUFFERS == 0 and nb % MOBA_GROUP == 0
    hb = PROJ_BLOCK // HEAD_DIM

    def head_spec(pb):
        return pl.BlockSpec((1, HEAD_DIM, seq), lambda b, h: (b, pb * hb + h, 0))

    gk = MOBA_GROUP * MOBA_BLOCK
    return pl.pallas_call(
        functools.partial(_moba_kernel, n_steps=n_steps),
        grid=(batch, MOBA_HEADS),
        in_specs=[head_spec(PB_Q), head_spec(PB_K), head_spec(PB_V)],
        out_specs=pl.BlockSpec((1, HEAD_DIM, seq), lambda b, h: (b, h, 0)),
        out_shape=jax.ShapeDtypeStruct((batch, MOBA_W, seq), BF16),
        scratch_shapes=[
            pltpu.VMEM((seq, 2 * HEAD_DIM), BF16),
            pltpu.VMEM((nb, HEAD_DIM), F32),
            pltpu.VMEM((nb, seq), BF16),
            pltpu.VMEM((HEAD_DIM + ONES_ROWS, seq), BF16),
        ] + [pltpu.VMEM((gk, MOBA_BLOCK), F32)] * SCORE_BUFFERS,
        compiler_params=pltpu.CompilerParams(
            dimension_semantics=("arbitrary", "arbitrary"),
            vmem_limit_bytes=MOBA_VMEM_LIMIT),
        name="moba",
    )(proj_t, proj_t, proj_t)


def _tail_kernel(u_ref, v_ref, gg_ref, qm_ref, gm_ref, gmo_ref, ymo_ref, x_ref, mk_ref, mvt_ref,
                 wst_ref, bs_ref, wot_ref, lng_ref, lnb_ref, o_ref, y_scr, *, alpha):
    tm = x_ref.shape[0]
    ch = TAIL_CHUNK
    nsub = ch // GMLP_CHUNK
    hd = HEAD_DIM
    rr = lax.broadcasted_iota(jnp.int32, (GMLP_CHUNK, GMLP_CHUNK), 0)
    cc = lax.broadcasted_iota(jnp.int32, (GMLP_CHUNK, GMLP_CHUNK), 1)

    for c in range(tm // ch):
        cs = slice(c * ch, (c + 1) * ch)
        y_scr[0:MOBA_W, cs] = ymo_ref[0, :, cs] * gmo_ref[0, :, cs]

        for g in range(GMLP_GROUPS):
            rows = slice(g * hd, (g + 1) * hd)
            w_t = jnp.where(rr <= cc, wst_ref[g], 0.0).astype(BF16)
            vg = v_ref[0, rows, cs]
            stacked = jnp.concatenate(
                [vg[:, n * GMLP_CHUNK:(n + 1) * GMLP_CHUNK] for n in range(nsub)], axis=0)
            mixed = _dot(stacked, w_t)
            mixed_t = jnp.concatenate(
                [mixed[n * hd:(n + 1) * hd, :] for n in range(nsub)], axis=1)
            mixed_t = mixed_t + _lane_tile(bs_ref[g:g + 1, :], nsub)
            yg = u_ref[0, rows, cs].astype(F32) * mixed_t * gg_ref[0, rows, cs].astype(F32)
            y_scr[MOBA_W + g * hd:MOBA_W + (g + 1) * hd, cs] = yg.astype(BF16)

        for h in range(MEM_HEADS):
            rows = slice(h * hd, (h + 1) * hd)
            s = _dot(mk_ref[0, :, rows], qm_ref[0, rows, cs])
            m = jnp.max(s, axis=0, keepdims=True)
            p = jnp.exp2(s - m)
            l = jnp.sum(p, axis=0, keepdims=True)
            o = _dot(mvt_ref[0, rows, :], p.astype(BF16))
            ym = o / l * gm_ref[0, rows, cs].astype(F32)
            y_scr[MOBA_W + GMLP_W + h * hd:MOBA_W + GMLP_W + (h + 1) * hd, cs] = ym.astype(BF16)

        sub = _dot(wot_ref[...], y_scr[:, cs]).T
        z = alpha * x_ref[cs, :] + sub
        mu = jnp.mean(z, axis=-1, keepdims=True)
        d = z - mu
        var = jnp.mean(d * d, axis=-1, keepdims=True)
        o_ref[cs, :] = d * lax.rsqrt(var + LN_EPS) * lng_ref[...] + lnb_ref[...]


def _tail(proj_t, ymo_t, x2, mem_k, mem_vt, ws_t, b_s, wo_t, ln_g, ln_b, batch, seq, alpha):
    tm = TAIL_ROWS
    nt = seq // tm

    def proj_spec(pb):
        return pl.BlockSpec((1, PROJ_BLOCK, tm), lambda i: (i // nt, pb, i % nt))

    return pl.pallas_call(
        functools.partial(_tail_kernel, alpha=alpha),
        grid=(batch * nt,),
        in_specs=[
            proj_spec(PB_U_GM), proj_spec(PB_V_GM), proj_spec(PB_G_GM),
            proj_spec(PB_Q_ME), proj_spec(PB_G_ME),
            pl.BlockSpec((1, MOBA_W, tm), lambda i: (i // nt, PB_G * PROJ_BLOCK // MOBA_W, i % nt)),
            pl.BlockSpec((1, MOBA_W, tm), lambda i: (i // nt, 0, i % nt)),
            pl.BlockSpec((tm, D_MODEL), lambda i: (i, 0)),
            pl.BlockSpec((1, MEM_LEN, MEM_W), lambda i: (i // nt, 0, 0)),
            pl.BlockSpec((1, MEM_W, MEM_LEN), lambda i: (i // nt, 0, 0)),
            pl.BlockSpec((GMLP_GROUPS, GMLP_CHUNK, GMLP_CHUNK), lambda i: (0, 0, 0)),
            pl.BlockSpec((GMLP_GROUPS, GMLP_CHUNK), lambda i: (0, 0)),
            pl.BlockSpec((D_MODEL, D_MIX), lambda i: (0, 0), pipeline_mode=pl.Buffered(1)),
            pl.BlockSpec((1, D_MODEL), lambda i: (0, 0)),
            pl.BlockSpec((1, D_MODEL), lambda i: (0, 0)),
        ],
        out_specs=pl.BlockSpec((tm, D_MODEL), lambda i: (i, 0)),
        out_shape=jax.ShapeDtypeStruct((batch * seq, D_MODEL), F32),
        scratch_shapes=[pltpu.VMEM((D_MIX, tm), BF16)],
        compiler_params=pltpu.CompilerParams(
            dimension_semantics=("arbitrary",), vmem_limit_bytes=VMEM_LIMIT),
        name="tail",
    )(proj_t, proj_t, proj_t, proj_t, proj_t, proj_t, ymo_t, x2, mem_k, mem_vt, ws_t, b_s, wo_t,
      ln_g, ln_b)


def kernel(x, mem, positions, w_in, w_mem_kv, gmlp_ln_g, gmlp_ln_b, gmlp_w_s, gmlp_b_s,
           w_out, ln_g, ln_b):
    batch, seq, d_model = x.shape
    depth = w_in.shape[0]
    assert d_model == D_MODEL and w_in.shape[2] == D_IN
    assert seq % INPROJ_ROWS == 0 and seq % MOBA_BLOCK == 0 and seq % TAIL_ROWS == 0
    alpha = float((2.0 * depth) ** 0.25)

    half = HEAD_DIM // 2
    inv_freq = ROPE_THETA ** (-jnp.arange(half, dtype=F32) / half)
    inv_freq = jnp.broadcast_to(inv_freq[:, None], (half, LANES))
    pos3 = positions.reshape(batch * seq // INPROJ_ROWS, 1, INPROJ_ROWS)

    x2 = x.reshape(batch * seq, d_model)
    for l in range(depth):
        w_t = w_in[l].T.astype(BF16)
        wk = w_mem_kv[l][:, :MEM_W].astype(BF16)
        wvt = w_mem_kv[l][:, MEM_W:].T.astype(BF16)
        wo_t = w_out[l].T.astype(BF16)
        ws_t = jnp.swapaxes(gmlp_w_s[l], 1, 2)
        lng = jnp.broadcast_to(gmlp_ln_g[l][:, None], (GMLP_W, LANES))
        lnb = jnp.broadcast_to(gmlp_ln_b[l][:, None], (GMLP_W, LANES))

        mem_k, mem_vt = _memkv(mem, wk, wvt)
        proj_t = _inproj(pos3, x2, w_t, inv_freq, lng, lnb, batch, seq)
        ymo_t = _moba(proj_t, batch, seq)
        x2 = _tail(proj_t, ymo_t, x2, mem_k, mem_vt, ws_t, gmlp_b_s[l], wo_t,
                   ln_g[l][None, :], ln_b[l][None, :], batch, seq, alpha)
    return x2.reshape(batch, seq, d_model)
```

```python
import functools

import jax
import jax.numpy as jnp
import numpy as np
from jax import lax
from jax.experimental import pallas as pl
from jax.experimental.pallas import tpu as pltpu

D_MODEL = 2048
HEAD_DIM = 128
MOBA_HEADS = 8
MOBA_W = MOBA_HEADS * HEAD_DIM
MOBA_BLOCK = 256
MOBA_TOPK = 3
GMLP_GROUPS = 4
GMLP_W = GMLP_GROUPS * HEAD_DIM
GMLP_CHUNK = 128
MEM_HEADS = 4
MEM_W = MEM_HEADS * HEAD_DIM
MEM_LEN = 256
D_MIX = MOBA_W + GMLP_W + MEM_W
D_IN = 4 * MOBA_W + 3 * GMLP_W + 2 * MEM_W
ROPE_THETA = 10000.0
LN_EPS = 1e-5
NEG_INF = -1e30
ATTN_SCALE = HEAD_DIM ** -0.5
LOG2E = float(np.log2(np.e))
Q_SCALE = ATTN_SCALE * LOG2E
MOBA_GROUP = 8
QK_PARTS = 2
SCORE_BUFFERS = 3
K_UNROLL = 8
GATE_TILES = 8
ONES_ROWS = 16

LANES = 128
PROJ_BLOCK = 512
N_PROJ_BLOCKS = D_IN // PROJ_BLOCK
PB_Q, PB_K, PB_V, PB_G = 0, 2, 4, 6
PB_U_GM, PB_V_GM, PB_G_GM, PB_Q_ME, PB_G_ME = 8, 9, 10, 11, 12

INPROJ_ROWS = 1024
INPROJ_CHUNK = 256
INPROJ_BUFFERS = 2
TAIL_ROWS = 512
TAIL_CHUNK = 256
VMEM_LIMIT = 48 * 1024 * 1024
MOBA_VMEM_LIMIT = 56 * 1024 * 1024

F32 = jnp.float32
BF16 = jnp.bfloat16

_NT = (((1,), (1,)), ((), ()))


def _dot(a, b):
    return jnp.dot(a, b, preferred_element_type=F32)


def _dot_nt(a, b):
    return lax.dot_general(a, b, _NT, preferred_element_type=F32)


def _lane_tile(a, reps):
    return a if reps == 1 else jnp.concatenate([a] * reps, axis=1)


def _silu(y):
    return y * (1.0 / (1.0 + jnp.exp(-y)))


def _gelu_tanh(y):
    c = np.float32(np.sqrt(2.0 / np.pi))
    return 0.5 * y * (1.0 + jnp.tanh(c * (y + 0.044715 * (y * y * y))))


def _memkv_kernel(mem_ref, wk_ref, wvt_ref, k_ref, vt_ref):
    m = mem_ref[0].astype(BF16)
    k_ref[0] = _dot(m, wk_ref[...]).astype(BF16)
    vt_ref[0] = _dot_nt(wvt_ref[...], m).astype(BF16)


def _memkv(mem, wk, wvt):
    b = mem.shape[0]
    return pl.pallas_call(
        _memkv_kernel,
        grid=(b,),
        in_specs=[
            pl.BlockSpec((1, MEM_LEN, D_MODEL), lambda i: (i, 0, 0)),
            pl.BlockSpec((D_MODEL, MEM_W), lambda i: (0, 0)),
            pl.BlockSpec((MEM_W, D_MODEL), lambda i: (0, 0)),
        ],
        out_specs=[
            pl.BlockSpec((1, MEM_LEN, MEM_W), lambda i: (i, 0, 0)),
            pl.BlockSpec((1, MEM_W, MEM_LEN), lambda i: (i, 0, 0)),
        ],
        out_shape=[
            jax.ShapeDtypeStruct((b, MEM_LEN, MEM_W), BF16),
            jax.ShapeDtypeStruct((b, MEM_W, MEM_LEN), BF16),
        ],
        compiler_params=pltpu.CompilerParams(
            dimension_semantics=("arbitrary",), vmem_limit_bytes=VMEM_LIMIT),
        name="memkv",
    )(mem, wk, wvt)


def _inproj_kernel(pos_ref, x_ref, w_ref, inv_ref, lng_ref, lnb_ref, o_ref,
                   xb_scr, cos_scr, sin_scr, *y_bufs):
    j = pl.program_id(1)
    tm = x_ref.shape[0]

    ch = INPROJ_CHUNK
    creps = ch // LANES

    def chunked(epilogue, first=False):
        n_chunks = tm // ch
        nbuf = len(y_bufs)

        def product(c):
            cs = slice(c * ch, (c + 1) * ch)
            if first:
                xb_scr[cs, :] = x_ref[cs, :].astype(BF16)
                ang = _lane_tile(inv_ref[...], creps) * pos_ref[0, :, cs].astype(F32)
                cos_scr[:, cs] = jnp.cos(ang)
                sin_scr[:, cs] = jnp.sin(ang)
            y_bufs[c % nbuf][...] = _dot_nt(w_ref[...], xb_scr[cs, :])

        for c in range(min(nbuf - 1, n_chunks)):
            product(c)
        for c in range(n_chunks):
            if c + nbuf - 1 < n_chunks:
                product(c + nbuf - 1)
            epilogue(y_bufs[c % nbuf][...], slice(c * ch, (c + 1) * ch))

    def rope_store(scale):
        def epilogue(y, cs):
            half = HEAD_DIM // 2
            c = cos_scr[:, cs]
            s = sin_scr[:, cs]
            for h in range(PROJ_BLOCK // HEAD_DIM):
                x1 = y[h * HEAD_DIM:h * HEAD_DIM + half]
                x2 = y[h * HEAD_DIM + half:(h + 1) * HEAD_DIM]
                o1 = x1 * c - x2 * s
                o2 = x2 * c + x1 * s
                if scale != 1.0:
                    o1 = o1 * scale
                    o2 = o2 * scale
                o_ref[0, h * HEAD_DIM:h * HEAD_DIM + half, cs] = o1.astype(BF16)
                o_ref[0, h * HEAD_DIM + half:(h + 1) * HEAD_DIM, cs] = o2.astype(BF16)
        return epilogue

    def store(fn):
        def epilogue(y, cs):
            o_ref[0, :, cs] = fn(y).astype(BF16)
        return epilogue

    def gelu_ln(y):
        v = _gelu_tanh(y)
        mu = jnp.mean(v, axis=0, keepdims=True)
        d = v - mu
        var = jnp.mean(d * d, axis=0, keepdims=True)
        n = d * lax.rsqrt(var + LN_EPS)
        return n * _lane_tile(lng_ref[...], creps) + _lane_tile(lnb_ref[...], creps)

    @pl.when(j == 0)
    def _():
        chunked(rope_store(Q_SCALE), first=True)

    @pl.when(jnp.logical_and(j > 0, j < PB_K))
    def _():
        chunked(rope_store(Q_SCALE))

    @pl.when(jnp.logical_and(j >= PB_K, j < PB_V))
    def _():
        chunked(rope_store(1.0))

    @pl.when(jnp.logical_and(j >= PB_V, j < PB_G))
    def _():
        chunked(store(lambda y: y))

    is_gate = jnp.logical_or(jnp.logical_and(j >= PB_G, j < PB_U_GM),
                             jnp.logical_or(j == PB_G_GM, j == PB_G_ME))

    @pl.when(is_gate)
    def _():
        chunked(store(_silu))

    @pl.when(j == PB_U_GM)
    def _():
        chunked(store(_gelu_tanh))

    @pl.when(j == PB_V_GM)
    def _():
        chunked(store(gelu_ln))

    @pl.when(j == PB_Q_ME)
    def _():
        chunked(store(lambda y: y * Q_SCALE))


def _inproj(pos3, x2, w_t, inv_freq, lng, lnb, batch, seq):
    tm = INPROJ_ROWS
    nt = seq // tm
    return pl.pallas_call(
        _inproj_kernel,
        grid=(batch * nt, N_PROJ_BLOCKS),
        in_specs=[
            pl.BlockSpec((1, 1, tm), lambda i, j: (i, 0, 0)),
            pl.BlockSpec((tm, D_MODEL), lambda i, j: (i, 0)),
            pl.BlockSpec((PROJ_BLOCK, D_MODEL), lambda i, j: (j, 0)),
            pl.BlockSpec((HEAD_DIM // 2, LANES), lambda i, j: (0, 0)),
            pl.BlockSpec((GMLP_W, LANES), lambda i, j: (0, 0)),
            pl.BlockSpec((GMLP_W, LANES), lambda i, j: (0, 0)),
        ],
        out_specs=pl.BlockSpec((1, PROJ_BLOCK, tm), lambda i, j: (i // nt, j, i % nt)),
        out_shape=jax.ShapeDtypeStruct((batch, D_IN, seq), BF16),
        scratch_shapes=[
            pltpu.VMEM((tm, D_MODEL), BF16),
            pltpu.VMEM((HEAD_DIM // 2, tm), F32),
            pltpu.VMEM((HEAD_DIM // 2, tm), F32),
        ] + [pltpu.VMEM((PROJ_BLOCK, INPROJ_CHUNK), F32)] * INPROJ_BUFFERS,
        compiler_params=pltpu.CompilerParams(
            dimension_semantics=("arbitrary", "arbitrary"), vmem_limit_bytes=VMEM_LIMIT),
        name="inproj",
    )(pos3, x2, w_t, inv_freq, lng, lnb)


def _moba_kernel(q_ref, kt_ref, vt_ref, o_ref, k_scr, km_scr, bias_scr, va_scr,
                 *s_bufs, n_steps):
    seq = kt_ref.shape[2]
    blk = MOBA_BLOCK
    nb = seq // blk
    gk = MOBA_GROUP * blk

    col = lax.broadcasted_iota(jnp.int32, (blk, LANES), 1)

    def k_body(c4, carry):
        for r in range(K_UNROLL):
            c = c4 * K_UNROLL + r
            start = pl.multiple_of(c * blk, blk)
            k = kt_ref[0, :, pl.ds(start, blk)].astype(F32).T
            k_scr[pl.ds(start, blk), 0:HEAD_DIM] = k.astype(BF16)
            k_scr[pl.ds(start, blk), HEAD_DIM:2 * HEAD_DIM] = jnp.where(col == c, 1.0, 0.0).astype(BF16)
            km_scr[pl.ds(c, 1), :] = jnp.sum(k, axis=0, keepdims=True) * (1.0 / blk)
        return carry

    lax.fori_loop(0, nb // K_UNROLL, k_body, 0)
    va_scr[0:HEAD_DIM, :] = vt_ref[0]
    va_scr[HEAD_DIM:HEAD_DIM + ONES_ROWS, :] = jnp.ones((ONES_ROWS, seq), BF16)

    km = km_scr[...]
    km_hi = km.astype(BF16)
    km_lo = (km - km_hi.astype(F32)).astype(BF16)
    gw = GATE_TILES * blk
    bid = lax.broadcasted_iota(jnp.int32, (nb, gw), 0)
    tile_in_chunk = lax.broadcasted_iota(jnp.int32, (nb, gw), 1) // blk

    def gate_body(c, carry):
        cols = pl.ds(pl.multiple_of(c * gw, gw), gw)
        qt = q_ref[0, :, cols]
        gate = _dot(km_hi, qt) + _dot(km_lo, qt)
        own_block = c * GATE_TILES + tile_in_chunk
        past = bid < own_block
        g = jnp.where(past, gate, NEG_INF)
        sel = jnp.zeros(gate.shape, F32)
        for _ in range(MOBA_TOPK):
            mx = jnp.max(g, axis=0, keepdims=True)
            first = jnp.min(jnp.where(g == mx, bid, nb), axis=0, keepdims=True)
            hit = bid == first
            sel = jnp.where(jnp.logical_and(hit, past), 1.0, sel)
            g = jnp.where(hit, -jnp.inf, g)
        bias = jnp.where(jnp.logical_or(sel > 0.0, bid == own_block), 0.0, NEG_INF)
        bias_scr[:, cols] = bias.astype(BF16)
        return carry

    lax.fori_loop(0, nb // GATE_TILES, gate_body, 0)

    def next_step(tile, pos):
        last = pos >= tile // MOBA_GROUP
        return (jnp.where(last, jnp.minimum(tile + 1, nb - 1), tile),
                jnp.where(last, 0, pos + 1))

    def group_start(tile, pos):
        g = jnp.where(pos == 0, tile // MOBA_GROUP, pos - 1)
        return pl.multiple_of(g * gk, gk)

    def tile_cols(tile):
        return pl.ds(pl.multiple_of(tile * blk, blk), blk)

    def score(step, s_ref):
        tile, pos = step
        cols = tile_cols(tile)
        parts = [q_ref[0, :, cols], bias_scr[:, cols]]
        if nb < HEAD_DIM:
            parts.append(jnp.zeros((HEAD_DIM - nb, blk), BF16))
        qa = jnp.concatenate(parts, axis=0)
        start = group_start(tile, pos)
        part = gk // QK_PARTS
        for r in range(QK_PARTS):
            s_ref[r * part:(r + 1) * part, :] = _dot(k_scr[pl.ds(start + r * part, part), :], qa)

    kk = lax.broadcasted_iota(jnp.int32, (blk, blk), 0)
    qq = lax.broadcasted_iota(jnp.int32, (blk, blk), 1)

    def absorb(step, s_ref, m, acc):
        tile, pos = step
        own = pl.ds(pl.multiple_of((tile % MOBA_GROUP) * blk, blk), blk)
        keep = kk <= qq + jnp.where(pos == 0, 0, blk)
        s_ref[own, :] = jnp.where(keep, s_ref[own, :], NEG_INF)
        s = s_ref[...]
        m_prev = jnp.minimum(m, jnp.where(pos == 0, NEG_INF, -NEG_INF).astype(F32))
        m_new = jnp.maximum(m_prev, jnp.max(s, axis=0, keepdims=True))
        p = jnp.exp2((s - m_new).astype(BF16))
        start = group_start(tile, pos)
        acc = jnp.exp2(m_prev - m_new) * acc + _dot(va_scr[:, pl.ds(start, gk)], p)
        cols = tile_cols(tile)
        l = acc[HEAD_DIM:HEAD_DIM + 1, :]
        o_ref[0, :, cols] = (acc[0:HEAD_DIM, :] / l).astype(BF16)
        return m_new, acc

    zero = jnp.int32(0)
    nbuf = len(s_bufs)
    ahead = [(zero, zero)]
    for _ in range(nbuf - 1):
        ahead.append(next_step(*ahead[-1]))
    for r in range(nbuf - 1):
        score(ahead[r], s_bufs[r])

    def rounds(u, carry):
        steps = [(carry[2 * r], carry[2 * r + 1]) for r in range(nbuf)]
        m, acc = carry[2 * nbuf:]
        for r in range(nbuf):
            score(steps[-1], s_bufs[(r + nbuf - 1) % nbuf])
            m, acc = absorb(steps[0], s_bufs[r], m, acc)
            steps = steps[1:] + [next_step(*steps[-1])]
        return tuple(x for st in steps for x in st) + (m, acc)

    init = tuple(x for st in ahead for x in st) + (
        jnp.full((1, blk), NEG_INF, F32), jnp.zeros((HEAD_DIM + ONES_ROWS, blk), F32))
    lax.fori_loop(0, n_steps // nbuf, rounds, init)


def _moba(proj_t, batch, seq):
    nb = seq // MOBA_BLOCK
    assert nb <= HEAD_DIM and nb % 16 == 0, "mask rows sit beside the head dim in one 256-deep contraction"
    n_steps = sum(i // MOBA_GROUP + 1 for i in range(nb))
    assert n_steps % SCORE_BUFFERS == 0 and nb % MOBA_GROUP == 0
    hb = PROJ_BLOCK // HEAD_DIM

    def head_spec(pb):
        return pl.BlockSpec((1, HEAD_DIM, seq), lambda b, h: (b, pb * hb + h, 0))

    gk = MOBA_GROUP * MOBA_BLOCK
    return pl.pallas_call(
        functools.partial(_moba_kernel, n_steps=n_steps),
        grid=(batch, MOBA_HEADS),
        in_specs=[head_spec(PB_Q), head_spec(PB_K), head_spec(PB_V)],
        out_specs=pl.BlockSpec((1, HEAD_DIM, seq), lambda b, h: (b, h, 0)),
        out_shape=jax.ShapeDtypeStruct((batch, MOBA_W, seq), BF16),
        scratch_shapes=[
            pltpu.VMEM((seq, 2 * HEAD_DIM), BF16),
            pltpu.VMEM((nb, HEAD_DIM), F32),
            pltpu.VMEM((nb, seq), BF16),
            pltpu.VMEM((HEAD_DIM + ONES_ROWS, seq), BF16),
        ] + [pltpu.VMEM((gk, MOBA_BLOCK), F32)] * SCORE_BUFFERS,
        compiler_params=pltpu.CompilerParams(
            dimension_semantics=("arbitrary", "arbitrary"),
            vmem_limit_bytes=MOBA_VMEM_LIMIT),
        name="moba",
    )(proj_t, proj_t, proj_t)


def _tail_kernel(u_ref, v_ref, gg_ref, qm_ref, gm_ref, gmo_ref, ymo_ref, x_ref, mk_ref, mvt_ref,
                 wst_ref, bs_ref, wo_ref, lng_ref, lnb_ref, o_ref, y_scr, *, alpha):
    tm = x_ref.shape[0]
    ch = TAIL_CHUNK
    nsub = ch // GMLP_CHUNK
    hd = HEAD_DIM
    rr = lax.broadcasted_iota(jnp.int32, (GMLP_CHUNK, GMLP_CHUNK), 0)
    cc = lax.broadcasted_iota(jnp.int32, (GMLP_CHUNK, GMLP_CHUNK), 1)

    for c in range(tm // ch):
        cs = slice(c * ch, (c + 1) * ch)
        y_scr[0:MOBA_W, cs] = ymo_ref[0, :, cs] * gmo_ref[0, :, cs]

        for g in range(GMLP_GROUPS):
            rows = slice(g * hd, (g + 1) * hd)
            w_t = jnp.where(rr <= cc, wst_ref[g], 0.0).astype(BF16)
            vg = v_ref[0, rows, cs]
            stacked = jnp.concatenate(
                [vg[:, n * GMLP_CHUNK:(n + 1) * GMLP_CHUNK] for n in range(nsub)], axis=0)
            mixed = _dot(stacked, w_t)
            mixed_t = jnp.concatenate(
                [mixed[n * hd:(n + 1) * hd, :] for n in range(nsub)], axis=1)
            mixed_t = mixed_t + _lane_tile(bs_ref[g:g + 1, :], nsub)
            yg = u_ref[0, rows, cs].astype(F32) * mixed_t * gg_ref[0, rows, cs].astype(F32)
            y_scr[MOBA_W + g * hd:MOBA_W + (g + 1) * hd, cs] = yg.astype(BF16)

        for h in range(MEM_HEADS):
            rows = slice(h * hd, (h + 1) * hd)
            s = _dot(mk_ref[0, :, rows], qm_ref[0, rows, cs])
            m = jnp.max(s, axis=0, keepdims=True)
            p = jnp.exp2(s - m)
            l = jnp.sum(p, axis=0, keepdims=True)
            o = _dot(mvt_ref[0, rows, :], p.astype(BF16))
            ym = o / l * gm_ref[0, rows, cs].astype(F32)
            y_scr[MOBA_W + GMLP_W + h * hd:MOBA_W + GMLP_W + (h + 1) * hd, cs] = ym.astype(BF16)

        sub = lax.dot_general(y_scr[:, cs], wo_ref[...], (((0,), (0,)), ((), ())),
                              preferred_element_type=F32)
        z = alpha * x_ref[cs, :] + sub
        mu = jnp.mean(z, axis=-1, keepdims=True)
        d = z - mu
        var = jnp.mean(d * d, axis=-1, keepdims=True)
        o_ref[cs, :] = d * lax.rsqrt(var + LN_EPS) * lng_ref[...] + lnb_ref[...]


def _tail(proj_t, ymo_t, x2, mem_k, mem_vt, ws_t, b_s, wo_b, ln_g, ln_b, batch, seq, alpha):
    tm = TAIL_ROWS
    nt = seq // tm

    def proj_spec(pb):
        return pl.BlockSpec((1, PROJ_BLOCK, tm), lambda i: (i // nt, pb, i % nt))

    return pl.pallas_call(
        functools.partial(_tail_kernel, alpha=alpha),
        grid=(batch * nt,),
        in_specs=[
            proj_spec(PB_U_GM), proj_spec(PB_V_GM), proj_spec(PB_G_GM),
            proj_spec(PB_Q_ME), proj_spec(PB_G_ME),
            pl.BlockSpec((1, MOBA_W, tm), lambda i: (i // nt, PB_G * PROJ_BLOCK // MOBA_W, i % nt)),
            pl.BlockSpec((1, MOBA_W, tm), lambda i: (i // nt, 0, i % nt)),
            pl.BlockSpec((tm, D_MODEL), lambda i: (i, 0)),
            pl.BlockSpec((1, MEM_LEN, MEM_W), lambda i: (i // nt, 0, 0)),
            pl.BlockSpec((1, MEM_W, MEM_LEN), lambda i: (i // nt, 0, 0)),
            pl.BlockSpec((GMLP_GROUPS, GMLP_CHUNK, GMLP_CHUNK), lambda i: (0, 0, 0)),
            pl.BlockSpec((GMLP_GROUPS, GMLP_CHUNK), lambda i: (0, 0)),
            pl.BlockSpec((D_MIX, D_MODEL), lambda i: (0, 0), pipeline_mode=pl.Buffered(1)),
            pl.BlockSpec((1, D_MODEL), lambda i: (0, 0)),
            pl.BlockSpec((1, D_MODEL), lambda i: (0, 0)),
        ],
        out_specs=pl.BlockSpec((tm, D_MODEL), lambda i: (i, 0)),
        out_shape=jax.ShapeDtypeStruct((batch * seq, D_MODEL), F32),
        scratch_shapes=[pltpu.VMEM((D_MIX, tm), BF16)],
        compiler_params=pltpu.CompilerParams(
            dimension_semantics=("arbitrary",), vmem_limit_bytes=VMEM_LIMIT),
        name="tail",
    )(proj_t, proj_t, proj_t, proj_t, proj_t, proj_t, ymo_t, x2, mem_k, mem_vt, ws_t, b_s, wo_b,
      ln_g, ln_b)


def kernel(x, mem, positions, w_in, w_mem_kv, gmlp_ln_g, gmlp_ln_b, gmlp_w_s, gmlp_b_s,
           w_out, ln_g, ln_b):
    batch, seq, d_model = x.shape
    depth = w_in.shape[0]
    assert d_model == D_MODEL and w_in.shape[2] == D_IN
    assert seq % INPROJ_ROWS == 0 and seq % MOBA_BLOCK == 0 and seq % TAIL_ROWS == 0
    alpha = float((2.0 * depth) ** 0.25)

    half = HEAD_DIM // 2
    inv_freq = ROPE_THETA ** (-jnp.arange(half, dtype=F32) / half)
    inv_freq = jnp.broadcast_to(inv_freq[:, None], (half, LANES))
    pos3 = positions.reshape(batch * seq // INPROJ_ROWS, 1, INPROJ_ROWS)

    x2 = x.reshape(batch * seq, d_model)
    for l in range(depth):
        w_t = w_in[l].T.astype(BF16)
        wk = w_mem_kv[l][:, :MEM_W].astype(BF16)
        wvt = w_mem_kv[l][:, MEM_W:].T.astype(BF16)
        wo_b = w_out[l].astype(BF16)
        ws_t = jnp.swapaxes(gmlp_w_s[l], 1, 2)
        lng = jnp.broadcast_to(gmlp_ln_g[l][:, None], (GMLP_W, LANES))
        lnb = jnp.broadcast_to(gmlp_ln_b[l][:, None], (GMLP_W, LANES))

        mem_k, mem_vt = _memkv(mem, wk, wvt)
        proj_t = _inproj(pos3, x2, w_t, inv_freq, lng, lnb, batch, seq)
        ymo_t = _moba(proj_t, batch, seq)
        x2 = _tail(proj_t, ymo_t, x2, mem_k, mem_vt, ws_t, gmlp_b_s[l], wo_b,
                   ln_g[l][None, :], ln_b[l][None, :], batch, seq, alpha)
    return x2.reshape(batch, seq, d_model)
```

```python
import functools

import jax
import jax.numpy as jnp
import numpy as np
from jax import lax
from jax.experimental import pallas as pl
from jax.experimental.pallas import tpu as pltpu

D_MODEL = 2048
HEAD_DIM = 128
MOBA_HEADS = 8
MOBA_W = MOBA_HEADS * HEAD_DIM
MOBA_BLOCK = 256
MOBA_TOPK = 3
GMLP_GROUPS = 4
GMLP_W = GMLP_GROUPS * HEAD_DIM
GMLP_CHUNK = 128
MEM_HEADS = 4
MEM_W = MEM_HEADS * HEAD_DIM
MEM_LEN = 256
D_MIX = MOBA_W + GMLP_W + MEM_W
D_IN = 4 * MOBA_W + 3 * GMLP_W + 2 * MEM_W
ROPE_THETA = 10000.0
LN_EPS = 1e-5
NEG_INF = -1e30
ATTN_SCALE = HEAD_DIM ** -0.5
LOG2E = float(np.log2(np.e))
Q_SCALE = ATTN_SCALE * LOG2E
MOBA_GROUP = 8
QK_PARTS = 2
SCORE_BUFFERS = 3
K_UNROLL = 8
GATE_TILES = 8
ONES_ROWS = 16

LANES = 128
PROJ_BLOCK = 512
N_PROJ_BLOCKS = D_IN // PROJ_BLOCK
PB_Q, PB_K, PB_V, PB_G = 0, 2, 4, 6
PB_U_GM, PB_V_GM, PB_G_GM, PB_Q_ME, PB_G_ME = 8, 9, 10, 11, 12

INPROJ_ROWS = 1024
INPROJ_CHUNK = 256
INPROJ_BUFFERS = 2
TAIL_ROWS = 512
TAIL_CHUNK = 256
VMEM_LIMIT = 48 * 1024 * 1024
MOBA_VMEM_LIMIT = 56 * 1024 * 1024

F32 = jnp.float32
BF16 = jnp.bfloat16

_NT = (((1,), (1,)), ((), ()))


def _dot(a, b):
    return jnp.dot(a, b, preferred_element_type=F32)


def _dot_nt(a, b):
    return lax.dot_general(a, b, _NT, preferred_element_type=F32)


def _lane_tile(a, reps):
    return a if reps == 1 else jnp.concatenate([a] * reps, axis=1)


def _silu(y):
    return y * (1.0 / (1.0 + jnp.exp(-y)))


def _gelu_tanh(y):
    c = np.float32(np.sqrt(2.0 / np.pi))
    return 0.5 * y * (1.0 + jnp.tanh(c * (y + 0.044715 * (y * y * y))))


def _memkv_kernel(mem_ref, wk_ref, wvt_ref, k_ref, vt_ref):
    m = mem_ref[0].astype(BF16)
    k_ref[0] = _dot(m, wk_ref[...]).astype(BF16)
    vt_ref[0] = _dot_nt(wvt_ref[...], m).astype(BF16)


def _memkv(mem, wk, wvt):
    b = mem.shape[0]
    return pl.pallas_call(
        _memkv_kernel,
        grid=(b,),
        in_specs=[
            pl.BlockSpec((1, MEM_LEN, D_MODEL), lambda i: (i, 0, 0)),
            pl.BlockSpec((D_MODEL, MEM_W), lambda i: (0, 0)),
            pl.BlockSpec((MEM_W, D_MODEL), lambda i: (0, 0)),
        ],
        out_specs=[
            pl.BlockSpec((1, MEM_LEN, MEM_W), lambda i: (i, 0, 0)),
            pl.BlockSpec((1, MEM_W, MEM_LEN), lambda i: (i, 0, 0)),
        ],
        out_shape=[
            jax.ShapeDtypeStruct((b, MEM_LEN, MEM_W), BF16),
            jax.ShapeDtypeStruct((b, MEM_W, MEM_LEN), BF16),
        ],
        compiler_params=pltpu.CompilerParams(
            dimension_semantics=("arbitrary",), vmem_limit_bytes=VMEM_LIMIT),
        name="memkv",
    )(mem, wk, wvt)


def _inproj_kernel(pos_ref, x_ref, w_ref, inv_ref, lng_ref, lnb_ref, o_ref,
                   xb_scr, cos_scr, sin_scr, *y_bufs):
    j = pl.program_id(1)
    tm = x_ref.shape[0]

    ch = INPROJ_CHUNK
    creps = ch // LANES

    def chunked(epilogue, first=False):
        n_chunks = tm // ch
        nbuf = len(y_bufs)

        def product(c):
            cs = slice(c * ch, (c + 1) * ch)
            if first:
                xb_scr[cs, :] = x_ref[cs, :].astype(BF16)
                ang = _lane_tile(inv_ref[...], creps) * pos_ref[0, :, cs].astype(F32)
                cos_scr[:, cs] = jnp.cos(ang)
                sin_scr[:, cs] = jnp.sin(ang)
            y_bufs[c % nbuf][...] = _dot_nt(w_ref[...], xb_scr[cs, :])

        for c in range(min(nbuf - 1, n_chunks)):
            product(c)
        for c in range(n_chunks):
            if c + nbuf - 1 < n_chunks:
                product(c + nbuf - 1)
            epilogue(y_bufs[c % nbuf][...], slice(c * ch, (c + 1) * ch))

    def rope_store(scale):
        def epilogue(y, cs):
            half = HEAD_DIM // 2
            c = cos_scr[:, cs]
            s = sin_scr[:, cs]
            for h in range(PROJ_BLOCK // HEAD_DIM):
                x1 = y[h * HEAD_DIM:h * HEAD_DIM + half]
                x2 = y[h * HEAD_DIM + half:(h + 1) * HEAD_DIM]
                o1 = x1 * c - x2 * s
                o2 = x2 * c + x1 * s
                if scale != 1.0:
                    o1 = o1 * scale
                    o2 = o2 * scale
                o_ref[0, h * HEAD_DIM:h * HEAD_DIM + half, cs] = o1.astype(BF16)
                o_ref[0, h * HEAD_DIM + half:(h + 1) * HEAD_DIM, cs] = o2.astype(BF16)
        return epilogue

    def store(fn):
        def epilogue(y, cs):
            o_ref[0, :, cs] = fn(y).astype(BF16)
        return epilogue

    def gelu_ln(y):
        v = _gelu_tanh(y)
        mu = jnp.mean(v, axis=0, keepdims=True)
        d = v - mu
        var = jnp.mean(d * d, axis=0, keepdims=True)
        n = d * lax.rsqrt(var + LN_EPS)
        return n * _lane_tile(lng_ref[...], creps) + _lane_tile(lnb_ref[...], creps)

    @pl.when(j == 0)
    def _():
        chunked(rope_store(Q_SCALE), first=True)

    @pl.when(jnp.logical_and(j > 0, j < PB_K))
    def _():
        chunked(rope_store(Q_SCALE))

    @pl.when(jnp.logical_and(j >= PB_K, j < PB_V))
    def _():
        chunked(rope_store(1.0))

    @pl.when(jnp.logical_and(j >= PB_V, j < PB_G))
    def _():
        chunked(store(lambda y: y))

    is_gate = jnp.logical_or(jnp.logical_and(j >= PB_G, j < PB_U_GM),
                             jnp.logical_or(j == PB_G_GM, j == PB_G_ME))

    @pl.when(is_gate)
    def _():
        chunked(store(_silu))

    @pl.when(j == PB_U_GM)
    def _():
        chunked(store(_gelu_tanh))

    @pl.when(j == PB_V_GM)
    def _():
        chunked(store(gelu_ln))

    @pl.when(j == PB_Q_ME)
    def _():
        chunked(store(lambda y: y * Q_SCALE))


def _inproj(pos3, x2, w_t, inv_freq, lng, lnb, batch, seq):
    tm = INPROJ_ROWS
    nt = seq // tm
    return pl.pallas_call(
        _inproj_kernel,
        grid=(batch * nt, N_PROJ_BLOCKS),
        in_specs=[
            pl.BlockSpec((1, 1, tm), lambda i, j: (i, 0, 0)),
            pl.BlockSpec((tm, D_MODEL), lambda i, j: (i, 0)),
            pl.BlockSpec((PROJ_BLOCK, D_MODEL), lambda i, j: (j, 0)),
            pl.BlockSpec((HEAD_DIM // 2, LANES), lambda i, j: (0, 0)),
            pl.BlockSpec((GMLP_W, LANES), lambda i, j: (0, 0)),
            pl.BlockSpec((GMLP_W, LANES), lambda i, j: (0, 0)),
        ],
        out_specs=pl.BlockSpec((1, PROJ_BLOCK, tm), lambda i, j: (i // nt, j, i % nt)),
        out_shape=jax.ShapeDtypeStruct((batch, D_IN, seq), BF16),
        scratch_shapes=[
            pltpu.VMEM((tm, D_MODEL), BF16),
            pltpu.VMEM((HEAD_DIM // 2, tm), F32),
            pltpu.VMEM((HEAD_DIM // 2, tm), F32),
        ] + [pltpu.VMEM((PROJ_BLOCK, INPROJ_CHUNK), F32)] * INPROJ_BUFFERS,
        compiler_params=pltpu.CompilerParams(
            dimension_semantics=("arbitrary", "arbitrary"), vmem_limit_bytes=VMEM_LIMIT),
        name="inproj",
    )(pos3, x2, w_t, inv_freq, lng, lnb)


def _moba_kernel(q_ref, kt_ref, vt_ref, o_ref, k_scr, km_scr, bias_scr, va_scr,
                 *s_bufs, n_steps):
    seq = kt_ref.shape[2]
    blk = MOBA_BLOCK
    nb = seq // blk
    gk = MOBA_GROUP * blk

    col = lax.broadcasted_iota(jnp.int32, (blk, LANES), 1)

    def k_body(c4, carry):
        for r in range(K_UNROLL):
            c = c4 * K_UNROLL + r
            start = pl.multiple_of(c * blk, blk)
            k = kt_ref[0, :, pl.ds(start, blk)].astype(F32).T
            k_scr[pl.ds(start, blk), 0:HEAD_DIM] = k.astype(BF16)
            k_scr[pl.ds(start, blk), HEAD_DIM:2 * HEAD_DIM] = jnp.where(col == c, 1.0, 0.0).astype(BF16)
            km_scr[pl.ds(c, 1), :] = jnp.sum(k, axis=0, keepdims=True) * (1.0 / blk)
        return carry

    lax.fori_loop(0, nb // K_UNROLL, k_body, 0)
    va_scr[0:HEAD_DIM, :] = vt_ref[0]
    va_scr[HEAD_DIM:HEAD_DIM + ONES_ROWS, :] = jnp.ones((ONES_ROWS, seq), BF16)

    km = km_scr[...]
    km_hi = km.astype(BF16)
    km_lo = (km - km_hi.astype(F32)).astype(BF16)
    gw = GATE_TILES * blk
    bid = lax.broadcasted_iota(jnp.int32, (nb, gw), 0)
    tile_in_chunk = lax.broadcasted_iota(jnp.int32, (nb, gw), 1) // blk

    def gate_body(c, carry):
        cols = pl.ds(pl.multiple_of(c * gw, gw), gw)
        qt = q_ref[0, :, cols]
        gate = _dot(km_hi, qt) + _dot(km_lo, qt)
        own_block = c * GATE_TILES + tile_in_chunk
        past = bid < own_block
        g = jnp.where(past, gate, NEG_INF)
        sel = jnp.zeros(gate.shape, F32)
        for _ in range(MOBA_TOPK):
            mx = jnp.max(g, axis=0, keepdims=True)
            first = jnp.min(jnp.where(g == mx, bid, nb), axis=0, keepdims=True)
            hit = bid == first
            sel = jnp.where(jnp.logical_and(hit, past), 1.0, sel)
            g = jnp.where(hit, -jnp.inf, g)
        bias = jnp.where(jnp.logical_or(sel > 0.0, bid == own_block), 0.0, NEG_INF)
        bias_scr[:, cols] = bias.astype(BF16)
        return carry

    lax.fori_loop(0, nb // GATE_TILES, gate_body, 0)

    def next_step(tile, pos):
        last = pos >= tile // MOBA_GROUP
        return (jnp.where(last, jnp.minimum(tile + 1, nb - 1), tile),
                jnp.where(last, 0, pos + 1))

    def group_start(tile, pos):
        g = jnp.where(pos == 0, tile // MOBA_GROUP, pos - 1)
        return pl.multiple_of(g * gk, gk)

    def tile_cols(tile):
        return pl.ds(pl.multiple_of(tile * blk, blk), blk)

    def score(step, s_ref):
        tile, pos = step
        cols = tile_cols(tile)
        parts = [q_ref[0, :, cols], bias_scr[:, cols]]
        if nb < HEAD_DIM:
            parts.append(jnp.zeros((HEAD_DIM - nb, blk), BF16))
        qa = jnp.concatenate(parts, axis=0)
        start = group_start(tile, pos)
        part = gk // QK_PARTS
        for r in range(QK_PARTS):
            s_ref[r * part:(r + 1) * part, :] = _dot(k_scr[pl.ds(start + r * part, part), :], qa)

    kk = lax.broadcasted_iota(jnp.int32, (blk, blk), 0)
    qq = lax.broadcasted_iota(jnp.int32, (blk, blk), 1)

    def absorb(step, s_ref, m, acc):
        tile, pos = step
        own = pl.ds(pl.multiple_of((tile % MOBA_GROUP) * blk, blk), blk)
        keep = kk <= qq + jnp.where(pos == 0, 0, blk)
        s_ref[own, :] = jnp.where(keep, s_ref[own, :], NEG_INF)
        s = s_ref[...]
        m_prev = jnp.minimum(m, jnp.where(pos == 0, NEG_INF, -NEG_INF).astype(F32))
        m_new = jnp.maximum(m_prev, jnp.max(s, axis=0, keepdims=True))
        p = jnp.exp2(s - m_new).astype(BF16)
        start = group_start(tile, pos)
        acc = jnp.exp2(m_prev - m_new) * acc + _dot(va_scr[:, pl.ds(start, gk)], p)
        cols = tile_cols(tile)
        l = acc[HEAD_DIM:HEAD_DIM + 1, :]
        o_ref[0, :, cols] = (acc[0:HEAD_DIM, :] / l).astype(BF16)
        return m_new, acc

    zero = jnp.int32(0)
    nbuf = len(s_bufs)
    ahead = [(zero, zero)]
    for _ in range(nbuf - 1):
        ahead.append(next_step(*ahead[-1]))
    for r in range(nbuf - 1):
        score(ahead[r], s_bufs[r])

    def rounds(u, carry):
        steps = [(carry[2 * r], carry[2 * r + 1]) for r in range(nbuf)]
        m, acc = carry[2 * nbuf:]
        for r in range(nbuf):
            score(steps[-1], s_bufs[(r + nbuf - 1) % nbuf])
            m, acc = absorb(steps[0], s_bufs[r], m, acc)
            steps = steps[1:] + [next_step(*steps[-1])]
        return tuple(x for st in steps for x in st) + (m, acc)

    init = tuple(x for st in ahead for x in st) + (
        jnp.full((1, blk), NEG_INF, F32), jnp.zeros((HEAD_DIM + ONES_ROWS, blk), F32))
    lax.fori_loop(0, n_steps // nbuf, rounds, init)


def _moba(proj_t, batch, seq):
    nb = seq // MOBA_BLOCK
    assert nb <= HEAD_DIM and nb % 16 == 0, "mask rows sit beside the head dim in one 256-deep contraction"
    n_steps = sum(i // MOBA_GROUP + 1 for i in range(nb))
    assert n_steps % SCORE_BUFFERS == 0 and nb % MOBA_GROUP == 0
    hb = PROJ_BLOCK // HEAD_DIM

    def head_spec(pb):
        return pl.BlockSpec((1, HEAD_DIM, seq), lambda b, h: (b, pb * hb + h, 0))

    gk = MOBA_GROUP * MOBA_BLOCK
    return pl.pallas_call(
        functools.partial(_moba_kernel, n_steps=n_steps),
        grid=(batch, MOBA_HEADS),
        in_specs=[head_spec(PB_Q), head_spec(PB_K), head_spec(PB_V)],
        out_specs=pl.BlockSpec((1, HEAD_DIM, seq), lambda b, h: (b, h, 0)),
        out_shape=jax.ShapeDtypeStruct((batch, MOBA_W, seq), BF16),
        scratch_shapes=[
            pltpu.VMEM((seq, 2 * HEAD_DIM), BF16),
            pltpu.VMEM((nb, HEAD_DIM), F32),
            pltpu.VMEM((nb, seq), BF16),
            pltpu.VMEM((HEAD_DIM + ONES_ROWS, seq), BF16),
        ] + [pltpu.VMEM((gk, MOBA_BLOCK), F32)] * SCORE_BUFFERS,
        compiler_params=pltpu.CompilerParams(
            dimension_semantics=("arbitrary", "arbitrary"),
            vmem_limit_bytes=MOBA_VMEM_LIMIT),
        name="moba",
    )(proj_t, proj_t, proj_t)


def _tail_kernel(u_ref, v_ref, gg_ref, qm_ref, gm_ref, gmo_ref, ymo_ref, x_ref, mk_ref, mvt_ref,
                 wst_ref, bs_ref, wo_ref, lng_ref, lnb_ref, o_ref, y_scr, *, alpha):
    tm = x_ref.shape[0]
    ch = TAIL_CHUNK
    nsub = ch // GMLP_CHUNK
    hd = HEAD_DIM
    rr = lax.broadcasted_iota(jnp.int32, (GMLP_CHUNK, GMLP_CHUNK), 0)
    cc = lax.broadcasted_iota(jnp.int32, (GMLP_CHUNK, GMLP_CHUNK), 1)

    for c in range(tm // ch):
        cs = slice(c * ch, (c + 1) * ch)
        y_scr[0:MOBA_W, cs] = ymo_ref[0, :, cs] * gmo_ref[0, :, cs]

        for g in range(GMLP_GROUPS):
            rows = slice(g * hd, (g + 1) * hd)
            w_t = jnp.where(rr <= cc, wst_ref[g], 0.0).astype(BF16)
            vg = v_ref[0, rows, cs]
            stacked = jnp.concatenate(
                [vg[:, n * GMLP_CHUNK:(n + 1) * GMLP_CHUNK] for n in range(nsub)], axis=0)
            mixed = _dot(stacked, w_t)
            mixed_t = jnp.concatenate(
                [mixed[n * hd:(n + 1) * hd, :] for n in range(nsub)], axis=1)
            mixed_t = mixed_t + _lane_tile(bs_ref[g:g + 1, :], nsub)
            yg = u_ref[0, rows, cs].astype(F32) * mixed_t * gg_ref[0, rows, cs].astype(F32)
            y_scr[MOBA_W + g * hd:MOBA_W + (g + 1) * hd, cs] = yg.astype(BF16)

        for h in range(MEM_HEADS):
            rows = slice(h * hd, (h + 1) * hd)
            s = _dot(mk_ref[0, :, rows], qm_ref[0, rows, cs])
            m = jnp.max(s, axis=0, keepdims=True)
            p = jnp.exp2(s - m)
            l = jnp.sum(p, axis=0, keepdims=True)
            o = _dot(mvt_ref[0, rows, :], p.astype(BF16))
            ym = o / l * gm_ref[0, rows, cs].astype(F32)
            y_scr[MOBA_W + GMLP_W + h * hd:MOBA_W + GMLP_W + (h + 1) * hd, cs] = ym.astype(BF16)

        sub = lax.dot_general(y_scr[:, cs], wo_ref[...], (((0,), (0,)), ((), ())),
                              preferred_element_type=F32)
        z = alpha * x_ref[cs, :] + sub
        mu = jnp.mean(z, axis=-1, keepdims=True)
        d = z - mu
        var = jnp.mean(d * d, axis=-1, keepdims=True)
        o_ref[cs, :] = d * lax.rsqrt(var + LN_EPS) * lng_ref[...] + lnb_ref[...]


def _tail(proj_t, ymo_t, x2, mem_k, mem_vt, ws_t, b_s, wo_b, ln_g, ln_b, batch, seq, alpha):
    tm = TAIL_ROWS
    nt = seq // tm

    def proj_spec(pb):
        return pl.BlockSpec((1, PROJ_BLOCK, tm), lambda i: (i // nt, pb, i % nt))

    return pl.pallas_call(
        functools.partial(_tail_kernel, alpha=alpha),
        grid=(batch * nt,),
        in_specs=[
            proj_spec(PB_U_GM), proj_spec(PB_V_GM), proj_spec(PB_G_GM),
            proj_spec(PB_Q_ME), proj_spec(PB_G_ME),
            pl.BlockSpec((1, MOBA_W, tm), lambda i: (i // nt, PB_G * PROJ_BLOCK // MOBA_W, i % nt)),
            pl.BlockSpec((1, MOBA_W, tm), lambda i: (i // nt, 0, i % nt)),
            pl.BlockSpec((tm, D_MODEL), lambda i: (i, 0)),
            pl.BlockSpec((1, MEM_LEN, MEM_W), lambda i: (i // nt, 0, 0)),
            pl.BlockSpec((1, MEM_W, MEM_LEN), lambda i: (i // nt, 0, 0)),
            pl.BlockSpec((GMLP_GROUPS, GMLP_CHUNK, GMLP_CHUNK), lambda i: (0, 0, 0)),
            pl.BlockSpec((GMLP_GROUPS, GMLP_CHUNK), lambda i: (0, 0)),
            pl.BlockSpec((D_MIX, D_MODEL), lambda i: (0, 0), pipeline_mode=pl.Buffered(1)),
            pl.BlockSpec((1, D_MODEL), lambda i: (0, 0)),
            pl.BlockSpec((1, D_MODEL), lambda i: (0, 0)),
        ],
        out_specs=pl.BlockSpec((tm, D_MODEL), lambda i: (i, 0)),
        out_shape=jax.ShapeDtypeStruct((batch * seq, D_MODEL), F32),
        scratch_shapes=[pltpu.VMEM((D_MIX, tm), BF16)],
        compiler_params=pltpu.CompilerParams(
            dimension_semantics=("arbitrary",), vmem_limit_bytes=VMEM_LIMIT),
        name="tail",
    )(proj_t, proj_t, proj_t, proj_t, proj_t, proj_t, ymo_t, x2, mem_k, mem_vt, ws_t, b_s, wo_b,
      ln_g, ln_b)


def kernel(x, mem, positions, w_in, w_mem_kv, gmlp_ln_g, gmlp_ln_b, gmlp_w_s, gmlp_b_s,
           w_out, ln_g, ln_b):
    batch, seq, d_model = x.shape
    depth = w_in.shape[0]
    assert d_model == D_MODEL and w_in.shape[2] == D_IN
    assert seq % INPROJ_ROWS == 0 and seq % MOBA_BLOCK == 0 and seq % TAIL_ROWS == 0
    alpha = float((2.0 * depth) ** 0.25)

    half = HEAD_DIM // 2
    inv_freq = ROPE_THETA ** (-jnp.arange(half, dtype=F32) / half)
    inv_freq = jnp.broadcast_to(inv_freq[:, None], (half, LANES))
    pos3 = positions.reshape(batch * seq // INPROJ_ROWS, 1, INPROJ_ROWS)

    x2 = x.reshape(batch * seq, d_model)
    for l in range(depth):
        w_t = w_in[l].T.astype(BF16)
        wk = w_mem_kv[l][:, :MEM_W].astype(BF16)
        wvt = w_mem_kv[l][:, MEM_W:].T.astype(BF16)
        wo_b = w_out[l].astype(BF16)
        ws_t = jnp.swapaxes(gmlp_w_s[l], 1, 2)
        lng = jnp.broadcast_to(gmlp_ln_g[l][:, None], (GMLP_W, LANES))
        lnb = jnp.broadcast_to(gmlp_ln_b[l][:, None], (GMLP_W, LANES))

        mem_k, mem_vt = _memkv(mem, wk, wvt)
        proj_t = _inproj(pos3, x2, w_t, inv_freq, lng, lnb, batch, seq)
        ymo_t = _moba(proj_t, batch, seq)
        x2 = _tail(proj_t, ymo_t, x2, mem_k, mem_vt, ws_t, gmlp_b_s[l], wo_b,
                   ln_g[l][None, :], ln_b[l][None, :], batch, seq, alpha)
    return x2.reshape(batch, seq, d_model)
```

```python
import functools

import jax
import jax.numpy as jnp
import numpy as np
from jax import lax
from jax.experimental import pallas as pl
from jax.experimental.pallas import tpu as pltpu

D_MODEL = 2048
HEAD_DIM = 128
MOBA_HEADS = 8
MOBA_W = MOBA_HEADS * HEAD_DIM
MOBA_BLOCK = 256
MOBA_TOPK = 3
GMLP_GROUPS = 4
GMLP_W = GMLP_GROUPS * HEAD_DIM
GMLP_CHUNK = 128
MEM_HEADS = 4
MEM_W = MEM_HEADS * HEAD_DIM
MEM_LEN = 256
D_MIX = MOBA_W + GMLP_W + MEM_W
D_IN = 4 * MOBA_W + 3 * GMLP_W + 2 * MEM_W
ROPE_THETA = 10000.0
LN_EPS = 1e-5
NEG_INF = -1e30
ATTN_SCALE = HEAD_DIM ** -0.5
LOG2E = float(np.log2(np.e))
Q_SCALE = ATTN_SCALE * LOG2E
MOBA_GROUP = 8
QK_PARTS = 2
SCORE_BUFFERS = 4
K_UNROLL = 8
GATE_TILES = 8
ONES_ROWS = 16

LANES = 128
PROJ_BLOCK = 512
N_PROJ_BLOCKS = D_IN // PROJ_BLOCK
PB_Q, PB_K, PB_V, PB_G = 0, 2, 4, 6
PB_U_GM, PB_V_GM, PB_G_GM, PB_Q_ME, PB_G_ME = 8, 9, 10, 11, 12

INPROJ_ROWS = 1024
INPROJ_CHUNK = 256
INPROJ_BUFFERS = 2
TAIL_ROWS = 512
TAIL_CHUNK = 256
VMEM_LIMIT = 48 * 1024 * 1024
MOBA_VMEM_LIMIT = 56 * 1024 * 1024

F32 = jnp.float32
BF16 = jnp.bfloat16

_NT = (((1,), (1,)), ((), ()))


def _dot(a, b):
    return jnp.dot(a, b, preferred_element_type=F32)


def _dot_nt(a, b):
    return lax.dot_general(a, b, _NT, preferred_element_type=F32)


def _lane_tile(a, reps):
    return a if reps == 1 else jnp.concatenate([a] * reps, axis=1)


def _silu(y):
    return y * (1.0 / (1.0 + jnp.exp(-y)))


def _gelu_tanh(y):
    c = np.float32(np.sqrt(2.0 / np.pi))
    return 0.5 * y * (1.0 + jnp.tanh(c * (y + 0.044715 * (y * y * y))))


def _memkv_kernel(mem_ref, wk_ref, wvt_ref, k_ref, vt_ref):
    m = mem_ref[0].astype(BF16)
    k_ref[0] = _dot(m, wk_ref[...]).astype(BF16)
    vt_ref[0] = _dot_nt(wvt_ref[...], m).astype(BF16)


def _memkv(mem, wk, wvt):
    b = mem.shape[0]
    return pl.pallas_call(
        _memkv_kernel,
        grid=(b,),
        in_specs=[
            pl.BlockSpec((1, MEM_LEN, D_MODEL), lambda i: (i, 0, 0)),
            pl.BlockSpec((D_MODEL, MEM_W), lambda i: (0, 0)),
            pl.BlockSpec((MEM_W, D_MODEL), lambda i: (0, 0)),
        ],
        out_specs=[
            pl.BlockSpec((1, MEM_LEN, MEM_W), lambda i: (i, 0, 0)),
            pl.BlockSpec((1, MEM_W, MEM_LEN), lambda i: (i, 0, 0)),
        ],
        out_shape=[
            jax.ShapeDtypeStruct((b, MEM_LEN, MEM_W), BF16),
            jax.ShapeDtypeStruct((b, MEM_W, MEM_LEN), BF16),
        ],
        compiler_params=pltpu.CompilerParams(
            dimension_semantics=("arbitrary",), vmem_limit_bytes=VMEM_LIMIT),
        name="memkv",
    )(mem, wk, wvt)


def _inproj_kernel(pos_ref, x_ref, w_ref, inv_ref, lng_ref, lnb_ref, o_ref,
                   xb_scr, cos_scr, sin_scr, *y_bufs):
    j = pl.program_id(1)
    tm = x_ref.shape[0]

    ch = INPROJ_CHUNK
    creps = ch // LANES

    def chunked(epilogue, first=False):
        n_chunks = tm // ch
        nbuf = len(y_bufs)

        def product(c):
            cs = slice(c * ch, (c + 1) * ch)
            if first:
                xb_scr[cs, :] = x_ref[cs, :].astype(BF16)
                ang = _lane_tile(inv_ref[...], creps) * pos_ref[0, :, cs].astype(F32)
                cos_scr[:, cs] = jnp.cos(ang)
                sin_scr[:, cs] = jnp.sin(ang)
            y_bufs[c % nbuf][...] = _dot_nt(w_ref[...], xb_scr[cs, :])

        for c in range(min(nbuf - 1, n_chunks)):
            product(c)
        for c in range(n_chunks):
            if c + nbuf - 1 < n_chunks:
                product(c + nbuf - 1)
            epilogue(y_bufs[c % nbuf][...], slice(c * ch, (c + 1) * ch))

    def rope_store(scale):
        def epilogue(y, cs):
            half = HEAD_DIM // 2
            c = cos_scr[:, cs]
            s = sin_scr[:, cs]
            for h in range(PROJ_BLOCK // HEAD_DIM):
                x1 = y[h * HEAD_DIM:h * HEAD_DIM + half]
                x2 = y[h * HEAD_DIM + half:(h + 1) * HEAD_DIM]
                o1 = x1 * c - x2 * s
                o2 = x2 * c + x1 * s
                if scale != 1.0:
                    o1 = o1 * scale
                    o2 = o2 * scale
                o_ref[0, h * HEAD_DIM:h * HEAD_DIM + half, cs] = o1.astype(BF16)
                o_ref[0, h * HEAD_DIM + half:(h + 1) * HEAD_DIM, cs] = o2.astype(BF16)
        return epilogue

    def store(fn):
        def epilogue(y, cs):
            o_ref[0, :, cs] = fn(y).astype(BF16)
        return epilogue

    def gelu_ln(y):
        v = _gelu_tanh(y)
        mu = jnp.mean(v, axis=0, keepdims=True)
        d = v - mu
        var = jnp.mean(d * d, axis=0, keepdims=True)
        n = d * lax.rsqrt(var + LN_EPS)
        return n * _lane_tile(lng_ref[...], creps) + _lane_tile(lnb_ref[...], creps)

    @pl.when(j == 0)
    def _():
        chunked(rope_store(Q_SCALE), first=True)

    @pl.when(jnp.logical_and(j > 0, j < PB_K))
    def _():
        chunked(rope_store(Q_SCALE))

    @pl.when(jnp.logical_and(j >= PB_K, j < PB_V))
    def _():
        chunked(rope_store(1.0))

    @pl.when(jnp.logical_and(j >= PB_V, j < PB_G))
    def _():
        chunked(store(lambda y: y))

    is_gate = jnp.logical_or(jnp.logical_and(j >= PB_G, j < PB_U_GM),
                             jnp.logical_or(j == PB_G_GM, j == PB_G_ME))

    @pl.when(is_gate)
    def _():
        chunked(store(_silu))

    @pl.when(j == PB_U_GM)
    def _():
        chunked(store(_gelu_tanh))

    @pl.when(j == PB_V_GM)
    def _():
        chunked(store(gelu_ln))

    @pl.when(j == PB_Q_ME)
    def _():
        chunked(store(lambda y: y * Q_SCALE))


def _inproj(pos3, x2, w_t, inv_freq, lng, lnb, batch, seq):
    tm = INPROJ_ROWS
    nt = seq // tm
    return pl.pallas_call(
        _inproj_kernel,
        grid=(batch * nt, N_PROJ_BLOCKS),
        in_specs=[
            pl.BlockSpec((1, 1, tm), lambda i, j: (i, 0, 0)),
            pl.BlockSpec((tm, D_MODEL), lambda i, j: (i, 0)),
            pl.BlockSpec((PROJ_BLOCK, D_MODEL), lambda i, j: (j, 0)),
            pl.BlockSpec((HEAD_DIM // 2, LANES), lambda i, j: (0, 0)),
            pl.BlockSpec((GMLP_W, LANES), lambda i, j: (0, 0)),
            pl.BlockSpec((GMLP_W, LANES), lambda i, j: (0, 0)),
        ],
        out_specs=pl.BlockSpec((1, PROJ_BLOCK, tm), lambda i, j: (i // nt, j, i % nt)),
        out_shape=jax.ShapeDtypeStruct((batch, D_IN, seq), BF16),
        scratch_shapes=[
            pltpu.VMEM((tm, D_MODEL), BF16),
            pltpu.VMEM((HEAD_DIM // 2, tm), F32),
            pltpu.VMEM((HEAD_DIM // 2, tm), F32),
        ] + [pltpu.VMEM((PROJ_BLOCK, INPROJ_CHUNK), F32)] * INPROJ_BUFFERS,
        compiler_params=pltpu.CompilerParams(
            dimension_semantics=("arbitrary", "arbitrary"), vmem_limit_bytes=VMEM_LIMIT),
        name="inproj",
    )(pos3, x2, w_t, inv_freq, lng, lnb)


def _moba_kernel(q_ref, kt_ref, vt_ref, o_ref, k_scr, km_scr, bias_scr, va_scr,
                 *s_bufs, n_steps):
    seq = kt_ref.shape[2]
    blk = MOBA_BLOCK
    nb = seq // blk
    gk = MOBA_GROUP * blk

    col = lax.broadcasted_iota(jnp.int32, (blk, LANES), 1)

    def k_body(c4, carry):
        for r in range(K_UNROLL):
            c = c4 * K_UNROLL + r
            start = pl.multiple_of(c * blk, blk)
            k = kt_ref[0, :, pl.ds(start, blk)].astype(F32).T
            k_scr[pl.ds(start, blk), 0:HEAD_DIM] = k.astype(BF16)
            k_scr[pl.ds(start, blk), HEAD_DIM:2 * HEAD_DIM] = jnp.where(col == c, 1.0, 0.0).astype(BF16)
            km_scr[pl.ds(c, 1), :] = jnp.sum(k, axis=0, keepdims=True) * (1.0 / blk)
        return carry

    lax.fori_loop(0, nb // K_UNROLL, k_body, 0)
    va_scr[0:HEAD_DIM, :] = vt_ref[0]
    va_scr[HEAD_DIM:HEAD_DIM + ONES_ROWS, :] = jnp.ones((ONES_ROWS, seq), BF16)

    km = km_scr[...]
    km_hi = km.astype(BF16)
    km_lo = (km - km_hi.astype(F32)).astype(BF16)
    gw = GATE_TILES * blk
    bid = lax.broadcasted_iota(jnp.int32, (nb, gw), 0)
    tile_in_chunk = lax.broadcasted_iota(jnp.int32, (nb, gw), 1) // blk

    def gate_body(c, carry):
        cols = pl.ds(pl.multiple_of(c * gw, gw), gw)
        qt = q_ref[0, :, cols]
        gate = _dot(km_hi, qt) + _dot(km_lo, qt)
        own_block = c * GATE_TILES + tile_in_chunk
        past = bid < own_block
        g = jnp.where(past, gate, NEG_INF)
        sel = jnp.zeros(gate.shape, F32)
        for _ in range(MOBA_TOPK):
            mx = jnp.max(g, axis=0, keepdims=True)
            first = jnp.min(jnp.where(g == mx, bid, nb), axis=0, keepdims=True)
            hit = bid == first
            sel = jnp.where(jnp.logical_and(hit, past), 1.0, sel)
            g = jnp.where(hit, -jnp.inf, g)
        bias = jnp.where(jnp.logical_or(sel > 0.0, bid == own_block), 0.0, NEG_INF)
        bias_scr[:, cols] = bias.astype(BF16)
        return carry

    lax.fori_loop(0, nb // GATE_TILES, gate_body, 0)

    def next_step(tile, pos):
        last = pos >= tile // MOBA_GROUP
        return (jnp.where(last, jnp.minimum(tile + 1, nb - 1), tile),
                jnp.where(last, 0, pos + 1))

    def group_start(tile, pos):
        g = jnp.where(pos == 0, tile // MOBA_GROUP, pos - 1)
        return pl.multiple_of(g * gk, gk)

    def tile_cols(tile):
        return pl.ds(pl.multiple_of(tile * blk, blk), blk)

    def score(step, s_ref):
        tile, pos = step
        cols = tile_cols(tile)
        parts = [q_ref[0, :, cols], bias_scr[:, cols]]
        if nb < HEAD_DIM:
            parts.append(jnp.zeros((HEAD_DIM - nb, blk), BF16))
        qa = jnp.concatenate(parts, axis=0)
        start = group_start(tile, pos)
        part = gk // QK_PARTS
        for r in range(QK_PARTS):
            s_ref[r * part:(r + 1) * part, :] = _dot(k_scr[pl.ds(start + r * part, part), :], qa)

    kk = lax.broadcasted_iota(jnp.int32, (blk, blk), 0)
    qq = lax.broadcasted_iota(jnp.int32, (blk, blk), 1)

    def absorb(step, s_ref, m, acc):
        tile, pos = step
        own = pl.ds(pl.multiple_of((tile % MOBA_GROUP) * blk, blk), blk)
        keep = kk <= qq + jnp.where(pos == 0, 0, blk)
        s_ref[own, :] = jnp.where(keep, s_ref[own, :], NEG_INF)
        s = s_ref[...]
        m_prev = jnp.minimum(m, jnp.where(pos == 0, NEG_INF, -NEG_INF).astype(F32))
        m_new = jnp.maximum(m_prev, jnp.max(s, axis=0, keepdims=True))
        p = jnp.exp2((s - m_new).astype(BF16))
        start = group_start(tile, pos)
        acc = jnp.exp2(m_prev - m_new) * acc + _dot(va_scr[:, pl.ds(start, gk)], p)
        cols = tile_cols(tile)
        l = acc[HEAD_DIM:HEAD_DIM + 1, :]
        o_ref[0, :, cols] = (acc[0:HEAD_DIM, :] / l).astype(BF16)
        return m_new, acc

    zero = jnp.int32(0)
    nbuf = len(s_bufs)
    ahead = [(zero, zero)]
    for _ in range(nbuf - 1):
        ahead.append(next_step(*ahead[-1]))
    for r in range(nbuf - 1):
        score(ahead[r], s_bufs[r])

    def rounds(u, carry):
        steps = [(carry[2 * r], carry[2 * r + 1]) for r in range(nbuf)]
        m, acc = carry[2 * nbuf:]
        for r in range(nbuf):
            score(steps[-1], s_bufs[(r + nbuf - 1) % nbuf])
            m, acc = absorb(steps[0], s_bufs[r], m, acc)
            steps = steps[1:] + [next_step(*steps[-1])]
        return tuple(x for st in steps for x in st) + (m, acc)

    init = tuple(x for st in ahead for x in st) + (
        jnp.full((1, blk), NEG_INF, F32), jnp.zeros((HEAD_DIM + ONES_ROWS, blk), F32))
    lax.fori_loop(0, n_steps // nbuf, rounds, init)


def _moba(proj_t, batch, seq):
    nb = seq // MOBA_BLOCK
    assert nb <= HEAD_DIM and nb % 16 == 0, "mask rows sit beside the head dim in one 256-deep contraction"
    n_steps = sum(i // MOBA_GROUP + 1 for i in range(nb))
    assert n_steps % SCORE_BUFFERS == 0 and nb % MOBA_GROUP == 0
    hb = PROJ_BLOCK // HEAD_DIM

    def head_spec(pb):
        return pl.BlockSpec((1, HEAD_DIM, seq), lambda b, h: (b, pb * hb + h, 0))

    gk = MOBA_GROUP * MOBA_BLOCK
    return pl.pallas_call(
        functools.partial(_moba_kernel, n_steps=n_steps),
        grid=(batch, MOBA_HEADS),
        in_specs=[head_spec(PB_Q), head_spec(PB_K), head_spec(PB_V)],
        out_specs=pl.BlockSpec((1, HEAD_DIM, seq), lambda b, h: (b, h, 0)),
        out_shape=jax.ShapeDtypeStruct((batch, MOBA_W, seq), BF16),
        scratch_shapes=[
            pltpu.VMEM((seq, 2 * HEAD_DIM), BF16),
            pltpu.VMEM((nb, HEAD_DIM), F32),
            pltpu.VMEM((nb, seq), BF16),
            pltpu.VMEM((HEAD_DIM + ONES_ROWS, seq), BF16),
        ] + [pltpu.VMEM((gk, MOBA_BLOCK), F32)] * SCORE_BUFFERS,
        compiler_params=pltpu.CompilerParams(
            dimension_semantics=("arbitrary", "arbitrary"),
            vmem_limit_bytes=MOBA_VMEM_LIMIT),
        name="moba",
    )(proj_t, proj_t, proj_t)


def _tail_kernel(u_ref, v_ref, gg_ref, qm_ref, gm_ref, gmo_ref, ymo_ref, x_ref, mk_ref, mvt_ref,
                 wst_ref, bs_ref, wo_ref, lng_ref, lnb_ref, o_ref, y_scr, *, alpha):
    tm = x_ref.shape[0]
    ch = TAIL_CHUNK
    nsub = ch // GMLP_CHUNK
    hd = HEAD_DIM
    rr = lax.broadcasted_iota(jnp.int32, (GMLP_CHUNK, GMLP_CHUNK), 0)
    cc = lax.broadcasted_iota(jnp.int32, (GMLP_CHUNK, GMLP_CHUNK), 1)

    for c in range(tm // ch):
        cs = slice(c * ch, (c + 1) * ch)
        y_scr[0:MOBA_W, cs] = ymo_ref[0, :, cs] * gmo_ref[0, :, cs]

        for g in range(GMLP_GROUPS):
            rows = slice(g * hd, (g + 1) * hd)
            w_t = jnp.where(rr <= cc, wst_ref[g], 0.0).astype(BF16)
            vg = v_ref[0, rows, cs]
            stacked = jnp.concatenate(
                [vg[:, n * GMLP_CHUNK:(n + 1) * GMLP_CHUNK] for n in range(nsub)], axis=0)
            mixed = _dot(stacked, w_t)
            mixed_t = jnp.concatenate(
                [mixed[n * hd:(n + 1) * hd, :] for n in range(nsub)], axis=1)
            mixed_t = mixed_t + _lane_tile(bs_ref[g:g + 1, :], nsub)
            yg = u_ref[0, rows, cs].astype(F32) * mixed_t * gg_ref[0, rows, cs].astype(F32)
            y_scr[MOBA_W + g * hd:MOBA_W + (g + 1) * hd, cs] = yg.astype(BF16)

        for h in range(MEM_HEADS):
            rows = slice(h * hd, (h + 1) * hd)
            s = _dot(mk_ref[0, :, rows], qm_ref[0, rows, cs])
            m = jnp.max(s, axis=0, keepdims=True)
            p = jnp.exp2(s - m)
            l = jnp.sum(p, axis=0, keepdims=True)
            o = _dot(mvt_ref[0, rows, :], p.astype(BF16))
            ym = o / l * gm_ref[0, rows, cs].astype(F32)
            y_scr[MOBA_W + GMLP_W + h * hd:MOBA_W + GMLP_W + (h + 1) * hd, cs] = ym.astype(BF16)

        sub = lax.dot_general(y_scr[:, cs], wo_ref[...], (((0,), (0,)), ((), ())),
                              preferred_element_type=F32)
        z = alpha * x_ref[cs, :] + sub
        mu = jnp.mean(z, axis=-1, keepdims=True)
        d = z - mu
        var = jnp.mean(d * d, axis=-1, keepdims=True)
        o_ref[cs, :] = d * lax.rsqrt(var + LN_EPS) * lng_ref[...] + lnb_ref[...]


def _tail(proj_t, ymo_t, x2, mem_k, mem_vt, ws_t, b_s, wo_b, ln_g, ln_b, batch, seq, alpha):
    tm = TAIL_ROWS
    nt = seq // tm

    def proj_spec(pb):
        return pl.BlockSpec((1, PROJ_BLOCK, tm), lambda i: (i // nt, pb, i % nt))

    return pl.pallas_call(
        functools.partial(_tail_kernel, alpha=alpha),
        grid=(batch * nt,),
        in_specs=[
            proj_spec(PB_U_GM), proj_spec(PB_V_GM), proj_spec(PB_G_GM),
            proj_spec(PB_Q_ME), proj_spec(PB_G_ME),
            pl.BlockSpec((1, MOBA_W, tm), lambda i: (i // nt, PB_G * PROJ_BLOCK // MOBA_W, i % nt)),
            pl.BlockSpec((1, MOBA_W, tm), lambda i: (i // nt, 0, i % nt)),
            pl.BlockSpec((tm, D_MODEL), lambda i: (i, 0)),
            pl.BlockSpec((1, MEM_LEN, MEM_W), lambda i: (i // nt, 0, 0)),
            pl.BlockSpec((1, MEM_W, MEM_LEN), lambda i: (i // nt, 0, 0)),
            pl.BlockSpec((GMLP_GROUPS, GMLP_CHUNK, GMLP_CHUNK), lambda i: (0, 0, 0)),
            pl.BlockSpec((GMLP_GROUPS, GMLP_CHUNK), lambda i: (0, 0)),
            pl.BlockSpec((D_MIX, D_MODEL), lambda i: (0, 0), pipeline_mode=pl.Buffered(1)),
            pl.BlockSpec((1, D_MODEL), lambda i: (0, 0)),
            pl.BlockSpec((1, D_MODEL), lambda i: (0, 0)),
        ],
        out_specs=pl.BlockSpec((tm, D_MODEL), lambda i: (i, 0)),
        out_shape=jax.ShapeDtypeStruct((batch * seq, D_MODEL), F32),
        scratch_shapes=[pltpu.VMEM((D_MIX, tm), BF16)],
        compiler_params=pltpu.CompilerParams(
            dimension_semantics=("arbitrary",), vmem_limit_bytes=VMEM_LIMIT),
        name="tail",
    )(proj_t, proj_t, proj_t, proj_t, proj_t, proj_t, ymo_t, x2, mem_k, mem_vt, ws_t, b_s, wo_b,
      ln_g, ln_b)


def kernel(x, mem, positions, w_in, w_mem_kv, gmlp_ln_g, gmlp_ln_b, gmlp_w_s, gmlp_b_s,
           w_out, ln_g, ln_b):
    batch, seq, d_model = x.shape
    depth = w_in.shape[0]
    assert d_model == D_MODEL and w_in.shape[2] == D_IN
    assert seq % INPROJ_ROWS == 0 and seq % MOBA_BLOCK == 0 and seq % TAIL_ROWS == 0
    alpha = float((2.0 * depth) ** 0.25)

    half = HEAD_DIM // 2
    inv_freq = ROPE_THETA ** (-jnp.arange(half, dtype=F32) / half)
    inv_freq = jnp.broadcast_to(inv_freq[:, None], (half, LANES))
    pos3 = positions.reshape(batch * seq // INPROJ_ROWS, 1, INPROJ_ROWS)

    x2 = x.reshape(batch * seq, d_model)
    for l in range(depth):
        w_t = w_in[l].T.astype(BF16)
        wk = w_mem_kv[l][:, :MEM_W].astype(BF16)
        wvt = w_mem_kv[l][:, MEM_W:].T.astype(BF16)
        wo_b = w_out[l].astype(BF16)
        ws_t = jnp.swapaxes(gmlp_w_s[l], 1, 2)
        lng = jnp.broadcast_to(gmlp_ln_g[l][:, None], (GMLP_W, LANES))
        lnb = jnp.broadcast_to(gmlp_ln_b[l][:, None], (GMLP_W, LANES))

        mem_k, mem_vt = _memkv(mem, wk, wvt)
        proj_t = _inproj(pos3, x2, w_t, inv_freq, lng, lnb, batch, seq)
        ymo_t = _moba(proj_t, batch, seq)
        x2 = _tail(proj_t, ymo_t, x2, mem_k, mem_vt, ws_t, gmlp_b_s[l], wo_b,
                   ln_g[l][None, :], ln_b[l][None, :], batch, seq, alpha)
    return x2.reshape(batch, seq, d_model)
```
